```python
import math
import jax, jax.numpy as jnp
from jax import lax
import numpy as np

D_MODEL = 2048
BATCH = 1
SEQ = 16384
DEPTH = 1

D_HEAD = 64
N_ATTN_HEADS = D_MODEL // 256
D_VALUE = 2 * D_HEAD
ATTN_WIDTH = N_ATTN_HEADS * D_VALUE
CONV_WIDTH = D_MODEL // 2
CONV_SIZE = 3
Q_BLOCK = 128
ROPE_THETA = 10000.0
NORM_EPS = 1e-5
DEEPNORM_ALPHA = (2.0 * DEPTH) ** 0.25
DEEPNORM_BETA = (8.0 * DEPTH) ** -0.25
SEG_WIDTHS = (ATTN_WIDTH, ATTN_WIDTH, ATTN_WIDTH, ATTN_WIDTH,
              CONV_WIDTH, CONV_WIDTH, CONV_WIDTH, CONV_WIDTH,
              D_MODEL, D_MODEL)
IN_WIDTH = sum(SEG_WIDTHS)
SEG_SPLITS = tuple(int(v) for v in np.cumsum(SEG_WIDTHS)[:-1])

kernel_name = "hybrid_diffattn_shortconv_gated_deepnorm"


def _lambda_init(layer):
    return 0.8 - 0.6 * math.exp(-0.3 * layer)


def _rope_tables(positions):
    inv_freq = ROPE_THETA ** (-jnp.arange(0, D_HEAD, 2, dtype=jnp.float32) / D_HEAD)
    ang = positions.astype(jnp.float32)[..., None] * inv_freq
    return jnp.cos(ang)[:, :, None, None, :], jnp.sin(ang)[:, :, None, None, :]


def _apply_rope(t, cos, sin):
    t1, t2 = jnp.split(t.astype(jnp.float32), 2, axis=-1)
    return jnp.concatenate([t1 * cos - t2 * sin, t2 * cos + t1 * sin], axis=-1)


def _diff_attention(q, k, v, lam):
    B, S, H, _, dh = q.shape
    nb = S // Q_BLOCK
    vf = v.astype(jnp.float32)
    qb = (q * (dh ** -0.5)).reshape(B, nb, Q_BLOCK, H, 2, dh).transpose(1, 0, 2, 3, 4, 5)
    key_idx = jnp.arange(S, dtype=jnp.int32)

    def block(args):
        q_blk, start = args
        s = jnp.einsum('bqhmd,bkhmd->bhmqk', q_blk, k)
        q_idx = start + jnp.arange(Q_BLOCK, dtype=jnp.int32)
        causal = key_idx[None, :] <= q_idx[:, None]
        a = jax.nn.softmax(jnp.where(causal, s, -jnp.inf), axis=-1)
        p = a[:, :, 0] - lam * a[:, :, 1]
        return jnp.einsum('bhqk,bkhe->bqhe', p, vf)

    starts = jnp.arange(nb, dtype=jnp.int32) * Q_BLOCK
    o = lax.map(block, (qb, starts))
    return o.transpose(1, 0, 2, 3, 4).reshape(B, S, H, D_VALUE)


def _causal_dwconv(u, w):
    return lax.conv_general_dilated(
        u, w[:, None, :].astype(u.dtype), window_strides=(1,),
        padding=((CONV_SIZE - 1, 0),), dimension_numbers=('NWC', 'WIO', 'NWC'),
        feature_group_count=u.shape[-1])


def _layer(x, c, cos, sin, w_mod, b_mod, w_in, lq1, lk1, lq2, lk2, subln_gain,
           conv_w, w_proj_a, w_proj_b, w_out, ln_gain, ln_bias, lambda_init):
    B, S, D = x.shape
    mod = c @ w_mod + b_mod
    shift, scale, gate = jnp.split(mod, 3, axis=-1)
    u = x * (1.0 + scale[:, None, :]) + shift[:, None, :]

    proj = u @ w_in
    q, k, v, z_a, cb, cc, cx, z_b, g_a, g_b = jnp.split(proj, SEG_SPLITS, axis=-1)

    q = _apply_rope(q.reshape(B, S, N_ATTN_HEADS, 2, D_HEAD), cos, sin)
    k = _apply_rope(k.reshape(B, S, N_ATTN_HEADS, 2, D_HEAD), cos, sin)
    v = v.reshape(B, S, N_ATTN_HEADS, D_VALUE)
    lam = (jnp.exp(jnp.sum(lq1.astype(jnp.float32) * lk1.astype(jnp.float32)))
           - jnp.exp(jnp.sum(lq2.astype(jnp.float32) * lk2.astype(jnp.float32)))
           + lambda_init)
    o = _diff_attention(q, k, v, lam)
    o = o * lax.rsqrt(jnp.mean(o * o, axis=-1, keepdims=True) + NORM_EPS)
    o = o * subln_gain.astype(jnp.float32) * (1.0 - lambda_init)
    y_a = o.reshape(B, S, ATTN_WIDTH).astype(x.dtype) * jax.nn.silu(z_a)

    y_b = cb * _causal_dwconv(cc * cx, conv_w) * jax.nn.silu(z_b)

    h = jax.nn.sigmoid(g_a) * (y_a @ w_proj_a) + jax.nn.sigmoid(g_b) * (y_b @ w_proj_b)
    out = h @ w_out

    r = (DEEPNORM_ALPHA * x + gate[:, None, :] * out).astype(jnp.float32)
    mu = jnp.mean(r, axis=-1, keepdims=True)
    var = jnp.mean(jnp.square(r - mu), axis=-1, keepdims=True)
    y = (r - mu) * lax.rsqrt(var + NORM_EPS) * ln_gain.astype(jnp.float32) + ln_bias.astype(jnp.float32)
    return y.astype(x.dtype)


def setup_inputs(seed: int = 0) -> dict:
    key = jax.random.key(seed)
    ks = jax.random.split(key, 20)
    f32 = jnp.float32
    L, D = DEPTH, D_MODEL
    nrm = lambda k, shape: jax.random.normal(k, shape, f32)
    x = nrm(ks[0], (BATCH, SEQ, D))
    c = nrm(ks[1], (BATCH, D))
    positions = jnp.broadcast_to(jnp.arange(SEQ, dtype=jnp.int32), (BATCH, SEQ))
    w_mod = nrm(ks[2], (L, D, 3 * D)) * (0.2 * D ** -0.5)
    b_mod = nrm(ks[3], (L, 3 * D)) * 0.02
    col_scale = jnp.concatenate([
        jnp.full((w,), DEEPNORM_BETA if i in (2, 6) else 1.0, f32)
        for i, w in enumerate(SEG_WIDTHS)])
    w_in = nrm(ks[4], (L, D, IN_WIDTH)) * (D ** -0.5) * col_scale
    lq1 = nrm(ks[5], (L, D_HEAD)) * 0.1
    lk1 = nrm(ks[6], (L, D_HEAD)) * 0.1
    lq2 = nrm(ks[7], (L, D_HEAD)) * 0.1
    lk2 = nrm(ks[8], (L, D_HEAD)) * 0.1
    subln_gain = 1.0 + 0.02 * nrm(ks[9], (L, D_VALUE))
    conv_w = nrm(ks[10], (L, CONV_SIZE, CONV_WIDTH)) * (CONV_SIZE ** -0.5)
    w_proj_a = nrm(ks[11], (L, ATTN_WIDTH, D)) * (DEEPNORM_BETA * ATTN_WIDTH ** -0.5)
    w_proj_b = nrm(ks[12], (L, CONV_WIDTH, D)) * (DEEPNORM_BETA * CONV_WIDTH ** -0.5)
    w_out = nrm(ks[13], (L, D, D)) * (DEEPNORM_BETA * D ** -0.5)
    ln_gain = 1.0 + 0.02 * nrm(ks[14], (L, D))
    ln_bias = 0.02 * nrm(ks[15], (L, D))
    return {"x": x, "c": c, "positions": positions, "w_mod": w_mod, "b_mod": b_mod,
            "w_in": w_in, "lambda_q1": lq1, "lambda_k1": lk1, "lambda_q2": lq2,
            "lambda_k2": lk2, "subln_gain": subln_gain, "conv_w": conv_w,
            "w_proj_a": w_proj_a, "w_proj_b": w_proj_b, "w_out": w_out,
            "ln_gain": ln_gain, "ln_bias": ln_bias}


def reference(x, c, positions, w_mod, b_mod, w_in, lambda_q1, lambda_k1, lambda_q2,
              lambda_k2, subln_gain, conv_w, w_proj_a, w_proj_b, w_out, ln_gain, ln_bias):
    cos, sin = _rope_tables(positions)
    h = x
    for l in range(DEPTH):
        h = _layer(h, c, cos, sin, w_mod[l], b_mod[l], w_in[l], lambda_q1[l], lambda_k1[l],
                   lambda_q2[l], lambda_k2[l], subln_gain[l], conv_w[l], w_proj_a[l],
                   w_proj_b[l], w_out[l], ln_gain[l], ln_bias[l], _lambda_init(l))
    return h
```

```python
import functools
import math

import jax
import jax.numpy as jnp
from jax import lax
from jax.experimental import pallas as pl
from jax.experimental.pallas import tpu as pltpu

D_HEAD = 64
D_VALUE = 2 * D_HEAD
CONV_SIZE = 3
ROPE_THETA = 10000.0
NORM_EPS = 1e-5

LANES = 128
SUBLANES = 8
VMEM_LIMIT_BYTES = 56 * 1024 * 1024

F32 = jnp.float32
BF16 = jnp.bfloat16


def _lambda_init(layer):
    return 0.8 - 0.6 * math.exp(-0.3 * layer)


def _sigmoid(z):
    return 1.0 / (1.0 + jnp.exp(-z))


def _mod_kernel(c_ref, w_ref, b_ref, o_ref):
    o_ref[...] = jnp.sum(w_ref[...] * c_ref[...], axis=0, keepdims=True) + b_ref[...]


def _mod(c_col, w_mod, b_mod, tn=512):
    d, n = w_mod.shape
    return pl.pallas_call(
        _mod_kernel,
        out_shape=jax.ShapeDtypeStruct((1, n), F32),
        grid=(n // tn,),
        in_specs=[pl.BlockSpec((d, 1), lambda j: (0, 0)),
                  pl.BlockSpec((d, tn), lambda j: (0, j)),
                  pl.BlockSpec((1, tn), lambda j: (0, j))],
        out_specs=pl.BlockSpec((1, tn), lambda j: (0, j)),
        compiler_params=pltpu.CompilerParams(
            dimension_semantics=("arbitrary",), vmem_limit_bytes=VMEM_LIMIT_BYTES),
        name="mod",
    )(c_col, w_mod, b_mod)


def _inproj_kernel(x_ref, scale_ref, shift_ref, cos_ref, sin_ref, w_ref, o_ref, u_sc,
                   *, n_rope_tiles, slabs):
    j = pl.program_id(1)

    @pl.when(j == 0)
    def _():
        u_sc[...] = (x_ref[...] * (1.0 + scale_ref[...]) + shift_ref[...]).astype(BF16)

    acc = jnp.dot(u_sc[...], w_ref[...], preferred_element_type=F32)

    @pl.when(j < n_rope_tiles)
    def _():
        cos = cos_ref[...]
        sin = sin_ref[...]
        lane = lax.broadcasted_iota(jnp.int32, cos.shape, 1)
        first_half = (lane % D_HEAD) < (D_HEAD // 2)
        qscale = jnp.where(j == 0, D_HEAD ** -0.5, 1.0).astype(F32)
        for s in range(slabs):
            t = acc[:, s * LANES:(s + 1) * LANES]
            partner = jnp.where(first_half,
                                pltpu.roll(t, LANES - D_HEAD // 2, axis=1),
                                pltpu.roll(t, D_HEAD // 2, axis=1))
            o_ref[s] = ((t * cos + partner * sin) * qscale).astype(BF16)

    @pl.when(j >= n_rope_tiles)
    def _():
        for s in range(slabs):
            o_ref[s] = acc[:, s * LANES:(s + 1) * LANES].astype(BF16)


def _in_proj(x2, scale, shift, cos_t, sin_t, w_in_bf, rope_width, tm=1024, tn=1024):
    s_len, d = x2.shape
    n = w_in_bf.shape[1]
    slabs = tn // LANES
    kern = functools.partial(_inproj_kernel, n_rope_tiles=rope_width // tn, slabs=slabs)
    return pl.pallas_call(
        kern,
        out_shape=jax.ShapeDtypeStruct((n // LANES, s_len, LANES), BF16),
        grid=(s_len // tm, n // tn),
        in_specs=[pl.BlockSpec((tm, d), lambda i, j: (i, 0)),
                  pl.BlockSpec((1, d), lambda i, j: (0, 0)),
                  pl.BlockSpec((1, d), lambda i, j: (0, 0)),
                  pl.BlockSpec((tm, LANES), lambda i, j: (i, 0)),
                  pl.BlockSpec((tm, LANES), lambda i, j: (i, 0)),
                  pl.BlockSpec((d, tn), lambda i, j: (0, j))],
        out_specs=pl.BlockSpec((slabs, tm, LANES), lambda i, j: (j, i, 0)),
        scratch_shapes=[pltpu.VMEM((tm, d), BF16)],
        compiler_params=pltpu.CompilerParams(
            dimension_semantics=("arbitrary", "arbitrary"), vmem_limit_bytes=VMEM_LIMIT_BYTES),
        name="in_proj",
    )(x2, scale, shift, cos_t, sin_t, w_in_bf)


def _attn_kernel(lq1_ref, lk1_ref, lq2_ref, lk2_ref, gain_ref, q_ref, k_ref, v_ref, z_ref,
                 o_ref, m_sc, l_sc, acc_sc, *, tq, tk, lambda_init):
    i = pl.program_id(1)
    q = q_ref[0]
    lane = lax.broadcasted_iota(jnp.int32, q.shape, 1)
    zero = jnp.zeros_like(q)
    qs = (jnp.where(lane < D_HEAD, q, zero), jnp.where(lane >= D_HEAD, q, zero))

    m_sc[...] = jnp.full(m_sc.shape, -jnp.inf, F32)
    l_sc[...] = jnp.zeros(l_sc.shape, F32)
    acc_sc[...] = jnp.zeros(acc_sc.shape, F32)

    def step(j, masked):
        k = k_ref[0, pl.ds(pl.multiple_of(j * tk, tk), tk), :]
        v = v_ref[0, pl.ds(pl.multiple_of(j * tk, tk), tk), :]
        if masked:
            row = lax.broadcasted_iota(jnp.int32, (tq, tk), 0) + i * tq
            col = lax.broadcasted_iota(jnp.int32, (tq, tk), 1) + j * tk
            keep = col <= row
        for mp in range(2):
            s = lax.dot_general(qs[mp], k, (((1,), (1,)), ((), ())),
                                preferred_element_type=F32)
            if masked:
                s = jnp.where(keep, s, -jnp.inf)
            m_old = m_sc[mp]
            m_new = jnp.maximum(m_old, jnp.max(s, axis=1, keepdims=True))
            p = jnp.exp(s - m_new)
            alpha = jnp.exp(m_old - m_new)
            l_sc[mp] = alpha * l_sc[mp] + jnp.sum(p, axis=1, keepdims=True)
            acc_sc[mp] = alpha * acc_sc[mp] + jnp.dot(p.astype(BF16), v,
                                                      preferred_element_type=F32)
            m_sc[mp] = m_new

    n_sub = tq // tk
    def body(j, carry):
        step(j, False)
        return carry
    lax.fori_loop(0, i * n_sub, body, 0)
    for d in range(n_sub):
        step(i * n_sub + d, True)

    lam = (jnp.exp(jnp.sum(lq1_ref[...] * lk1_ref[...], axis=1, keepdims=True))
           - jnp.exp(jnp.sum(lq2_ref[...] * lk2_ref[...], axis=1, keepdims=True))
           + lambda_init)
    o = acc_sc[0] / l_sc[0] - lam * (acc_sc[1] / l_sc[1])
    o = o * lax.rsqrt(jnp.mean(o * o, axis=1, keepdims=True) + NORM_EPS)
    o = o * gain_ref[...] * (1.0 - lambda_init)
    z = z_ref[0].astype(F32)
    o_ref[...] = (o * (z * _sigmoid(z))).astype(o_ref.dtype)


def _attention(proj, lq1, lk1, lq2, lk2, gain, n_heads, lambda_init, tq=512, tk=512):
    n_slabs, s_len, _ = proj.shape
    kern = functools.partial(_attn_kernel, tq=tq, tk=tk, lambda_init=lambda_init)
    vec = lambda n: pl.BlockSpec((1, n), lambda h, i: (0, 0))
    return pl.pallas_call(
        kern,
        out_shape=jax.ShapeDtypeStruct((s_len, n_heads * D_VALUE), BF16),
        grid=(n_heads, s_len // tq),
        in_specs=[vec(D_HEAD), vec(D_HEAD), vec(D_HEAD), vec(D_HEAD), vec(D_VALUE),
                  pl.BlockSpec((1, tq, LANES), lambda h, i: (h, i, 0)),
                  pl.BlockSpec((1, s_len, LANES), lambda h, i: (n_heads + h, 0, 0)),
                  pl.BlockSpec((1, s_len, LANES), lambda h, i: (2 * n_heads + h, 0, 0)),
                  pl.BlockSpec((1, tq, LANES), lambda h, i: (3 * n_heads + h, i, 0))],
        out_specs=pl.BlockSpec((tq, D_VALUE), lambda h, i: (i, h)),
        scratch_shapes=[pltpu.VMEM((2, tq, 1), F32), pltpu.VMEM((2, tq, 1), F32),
                        pltpu.VMEM((2, tq, D_VALUE), F32)],
        compiler_params=pltpu.CompilerParams(
            dimension_semantics=("arbitrary", "arbitrary"), vmem_limit_bytes=VMEM_LIMIT_BYTES),
        name="attn",
    )(lq1, lk1, lq2, lk2, gain, proj, proj, proj, proj)


def _out_kernel(ya_ref, cb_ref, cc_ref, cx_ref, zb_ref, hc_ref, hx_ref, ga_ref, gb_ref, x_ref,
                gate_ref, cw_ref, pa_ref, pb_ref, wo_ref, lng_ref, lnb_ref, o_ref,
                *, tm, conv_slabs, gate_slabs, alpha):
    i = pl.program_id(0)
    row = lax.broadcasted_iota(jnp.int32, (tm, LANES), 0)
    have_prev = (i > 0).astype(F32)
    yb = []
    for s in range(conv_slabs):
        u = cc_ref[s].astype(F32) * cx_ref[s].astype(F32)
        prev = hc_ref[s].astype(F32) * hx_ref[s].astype(F32) * have_prev
        p1 = prev[SUBLANES - 1:SUBLANES, :]
        p2 = prev[SUBLANES - 2:SUBLANES - 1, :]
        u1 = jnp.where(row == 0, p1, pltpu.roll(u, 1, axis=0))
        u2 = jnp.where(row == 0, p2, jnp.where(row == 1, p1, pltpu.roll(u, 2, axis=0)))
        w = cw_ref[:, s * LANES:(s + 1) * LANES]
        conv = w[0:1, :] * u2 + w[1:2, :] * u1 + w[2:3, :] * u
        z = zb_ref[s].astype(F32)
        yb.append((cb_ref[s].astype(F32) * conv * (z * _sigmoid(z))).astype(BF16))
    y_b = jnp.concatenate(yb, axis=1)
    ha = jnp.dot(ya_ref[...], pa_ref[...], preferred_element_type=F32)
    hb = jnp.dot(y_b, pb_ref[...], preferred_element_type=F32)
    g_a = jnp.concatenate([ga_ref[s] for s in range(gate_slabs)], axis=1).astype(F32)
    g_b = jnp.concatenate([gb_ref[s] for s in range(gate_slabs)], axis=1).astype(F32)
    h = (_sigmoid(g_a) * ha + _sigmoid(g_b) * hb).astype(BF16)
    out = jnp.dot(h, wo_ref[...], preferred_element_type=F32)
    r = alpha * x_ref[...] + gate_ref[...] * out
    mu = jnp.mean(r, axis=1, keepdims=True)
    rc = r - mu
    var = jnp.mean(rc * rc, axis=1, keepdims=True)
    o_ref[...] = rc * lax.rsqrt(var + NORM_EPS) * lng_ref[...] + lnb_ref[...]


def _out_stage(y_a, proj, x2, gate, conv_w, pa_bf, pb_bf, wo_bf, ln_g, ln_b, alpha,
               seg_off, tm=256):
    s_len, d = x2.shape
    conv_w_width = conv_w.shape[1]
    conv_slabs = conv_w_width // LANES
    gate_slabs = d // LANES
    off_cb, off_cc, off_cx, off_zb, off_ga, off_gb = [o // LANES for o in seg_off]
    kern = functools.partial(_out_kernel, tm=tm, conv_slabs=conv_slabs, gate_slabs=gate_slabs,
                             alpha=alpha)
    slab = lambda nsl, off: pl.BlockSpec((nsl, tm, LANES), lambda i: (off // nsl, i, 0))
    halo = lambda off: pl.BlockSpec(
        (conv_slabs, SUBLANES, LANES),
        lambda i: (off // conv_slabs, jnp.maximum(i * (tm // SUBLANES) - 1, 0), 0))
    const = lambda shape: pl.BlockSpec(shape, lambda i: (0, 0), pipeline_mode=pl.Buffered(1))
    return pl.pallas_call(
        kern,
        out_shape=jax.ShapeDtypeStruct((s_len, d), F32),
        grid=(s_len // tm,),
        in_specs=[pl.BlockSpec((tm, y_a.shape[1]), lambda i: (i, 0)),
                  slab(conv_slabs, off_cb), slab(conv_slabs, off_cc), slab(conv_slabs, off_cx),
                  slab(conv_slabs, off_zb), halo(off_cc), halo(off_cx),
                  slab(gate_slabs, off_ga), slab(gate_slabs, off_gb),
                  pl.BlockSpec((tm, d), lambda i: (i, 0)),
                  const((1, d)), const((CONV_SIZE, conv_w_width)),
                  const(pa_bf.shape), const(pb_bf.shape), const(wo_bf.shape),
                  const((1, d)), const((1, d))],
        out_specs=pl.BlockSpec((tm, d), lambda i: (i, 0)),
        compiler_params=pltpu.CompilerParams(
            dimension_semantics=("arbitrary",), vmem_limit_bytes=VMEM_LIMIT_BYTES),
        name="out_stage",
    )(y_a, proj, proj, proj, proj, proj, proj, proj, proj, x2, gate, conv_w, pa_bf, pb_bf, wo_bf,
      ln_g, ln_b)


def _rope_tables(positions_1d):
    inv_freq = ROPE_THETA ** (-jnp.arange(0, D_HEAD, 2, dtype=F32) / D_HEAD)
    ang = positions_1d.astype(F32)[:, None] * inv_freq
    cos, sin = jnp.cos(ang), jnp.sin(ang)
    reps = LANES // D_HEAD
    cos_t = jnp.tile(jnp.concatenate([cos, cos], axis=1), (1, reps))
    sin_t = jnp.tile(jnp.concatenate([-sin, sin], axis=1), (1, reps))
    return cos_t, sin_t


def kernel(x, c, positions, w_mod, b_mod, w_in, lambda_q1, lambda_k1, lambda_q2, lambda_k2,
           subln_gain, conv_w, w_proj_a, w_proj_b, w_out, ln_gain, ln_bias):
    batch, s_len, d = x.shape
    depth = w_mod.shape[0]
    attn_width = w_proj_a.shape[1]
    conv_width = w_proj_b.shape[1]
    n_heads = attn_width // D_VALUE
    alpha = (2.0 * depth) ** 0.25
    seg_off = (4 * attn_width, 4 * attn_width + conv_width, 4 * attn_width + 2 * conv_width,
               4 * attn_width + 3 * conv_width, 4 * attn_width + 4 * conv_width,
               4 * attn_width + 4 * conv_width + d)
    outs = []
    for b in range(batch):
        h = x[b]
        cos_t, sin_t = _rope_tables(positions[b])
        c_col = c[b].reshape(d, 1)
        for l in range(depth):
            mod = _mod(c_col, w_mod[l], b_mod[l].reshape(1, 3 * d))
            shift, scale, gate = mod[:, :d], mod[:, d:2 * d], mod[:, 2 * d:]
            proj = _in_proj(h, scale, shift, cos_t, sin_t, w_in[l].astype(BF16),
                            rope_width=2 * attn_width)
            y_a = _attention(proj, lambda_q1[l].reshape(1, D_HEAD), lambda_k1[l].reshape(1, D_HEAD),
                             lambda_q2[l].reshape(1, D_HEAD), lambda_k2[l].reshape(1, D_HEAD),
                             subln_gain[l].reshape(1, D_VALUE), n_heads, _lambda_init(l))
            h = _out_stage(y_a, proj, h, gate, conv_w[l], w_proj_a[l].astype(BF16),
                           w_proj_b[l].astype(BF16), w_out[l].astype(BF16),
                           ln_gain[l].reshape(1, d), ln_bias[l].reshape(1, d), alpha, seg_off)
        outs.append(h)
    return jnp.stack(outs, axis=0)
```

```python
import functools
import math

import jax
import jax.numpy as jnp
from jax import lax
from jax.experimental import pallas as pl
from jax.experimental.pallas import tpu as pltpu

D_HEAD = 64
D_VALUE = 2 * D_HEAD
CONV_SIZE = 3
ROPE_THETA = 10000.0
NORM_EPS = 1e-5

LANES = 128
SUBLANES = 8
VMEM_LIMIT_BYTES = 56 * 1024 * 1024
ATTN_TQ = 512
ATTN_TK = 512
ONES_ROWS = 16
LOG2_E = math.log2(math.e)

F32 = jnp.float32
BF16 = jnp.bfloat16


def _lambda_init(layer):
    return 0.8 - 0.6 * math.exp(-0.3 * layer)


def _sigmoid(z):
    return 1.0 / (1.0 + jnp.exp(-z))


def _mod_kernel(c_ref, w_ref, b_ref, o_ref):
    o_ref[...] = jnp.sum(w_ref[...] * c_ref[...], axis=0, keepdims=True) + b_ref[...]


def _mod(c_col, w_mod, b_mod, tn=512):
    d, n = w_mod.shape
    return pl.pallas_call(
        _mod_kernel,
        out_shape=jax.ShapeDtypeStruct((1, n), F32),
        grid=(n // tn,),
        in_specs=[pl.BlockSpec((d, 1), lambda j: (0, 0)),
                  pl.BlockSpec((d, tn), lambda j: (0, j)),
                  pl.BlockSpec((1, tn), lambda j: (0, j))],
        out_specs=pl.BlockSpec((1, tn), lambda j: (0, j)),
        compiler_params=pltpu.CompilerParams(
            dimension_semantics=("arbitrary",), vmem_limit_bytes=VMEM_LIMIT_BYTES),
        name="mod",
    )(c_col, w_mod, b_mod)


def _inproj_kernel(x_ref, scale_ref, shift_ref, cos_ref, sin_ref, w_ref, o_ref, qt_ref, vt_ref, u_sc,
                   *, slabs, tk):
    j = pl.program_id(1)
    tm = x_ref.shape[0]

    @pl.when(j == 0)
    def _():
        u_sc[...] = (x_ref[...] * (1.0 + scale_ref[...]) + shift_ref[...]).astype(BF16)

    acc = jnp.dot(u_sc[...], w_ref[...], preferred_element_type=F32)

    def rope(t):
        lane = lax.broadcasted_iota(jnp.int32, t.shape, 1)
        first_half = (lane % D_HEAD) < (D_HEAD // 2)
        partner = jnp.where(first_half,
                            pltpu.roll(t, LANES - D_HEAD // 2, axis=1),
                            pltpu.roll(t, D_HEAD // 2, axis=1))
        return t * cos_ref[...] + partner * sin_ref[...]

    @pl.when(j == 0)
    def _():
        for s in range(slabs):
            r = rope(acc[:, s * LANES:(s + 1) * LANES]) * (D_HEAD ** -0.5 * LOG2_E)
            o_ref[s] = r.astype(BF16)
            qt_ref[s] = r.T.astype(BF16)

    @pl.when(j == 1)
    def _():
        for s in range(slabs):
            o_ref[s] = rope(acc[:, s * LANES:(s + 1) * LANES]).astype(BF16)

    @pl.when(j == 2)
    def _():
        for s in range(slabs):
            t = acc[:, s * LANES:(s + 1) * LANES]
            o_ref[s] = t.astype(BF16)
            for cblk in range(tm // tk):
                vt_ref[s, cblk, :LANES, :] = t[cblk * tk:(cblk + 1) * tk, :].T.astype(BF16)
                vt_ref[s, cblk, LANES:, :] = jnp.ones((ONES_ROWS, tk), BF16)

    @pl.when(j >= 3)
    def _():
        for s in range(slabs):
            o_ref[s] = acc[:, s * LANES:(s + 1) * LANES].astype(BF16)


def _in_proj(x2, scale, shift, cos_t, sin_t, w_in_bf, attn_width, tk, tm=1024):
    s_len, d = x2.shape
    n = w_in_bf.shape[1]
    tn = attn_width
    slabs = tn // LANES
    kern = functools.partial(_inproj_kernel, slabs=slabs, tk=tk)
    return pl.pallas_call(
        kern,
        out_shape=(jax.ShapeDtypeStruct((n // LANES, s_len, LANES), BF16),
                   jax.ShapeDtypeStruct((slabs, LANES, s_len), BF16),
                   jax.ShapeDtypeStruct((slabs, s_len // tk, LANES + ONES_ROWS, tk), BF16)),
        grid=(s_len // tm, n // tn),
        in_specs=[pl.BlockSpec((tm, d), lambda i, j: (i, 0)),
                  pl.BlockSpec((1, d), lambda i, j: (0, 0)),
                  pl.BlockSpec((1, d), lambda i, j: (0, 0)),
                  pl.BlockSpec((tm, LANES), lambda i, j: (i, 0)),
                  pl.BlockSpec((tm, LANES), lambda i, j: (i, 0)),
                  pl.BlockSpec((d, tn), lambda i, j: (0, j))],
        out_specs=(pl.BlockSpec((slabs, tm, LANES), lambda i, j: (j, i, 0)),
                   pl.BlockSpec((slabs, LANES, tm), lambda i, j: (0, 0, i)),
                   pl.BlockSpec((slabs, tm // tk, LANES + ONES_ROWS, tk), lambda i, j: (0, i, 0, 0))),
        scratch_shapes=[pltpu.VMEM((tm, d), BF16)],
        compiler_params=pltpu.CompilerParams(
            dimension_semantics=("arbitrary", "arbitrary"), vmem_limit_bytes=VMEM_LIMIT_BYTES),
        name="in_proj",
    )(x2, scale, shift, cos_t, sin_t, w_in_bf)


def _attn_kernel(lq1_ref, lk1_ref, lq2_ref, lk2_ref, gain_ref, qt_ref, k_ref, vt_ref, z_ref,
                 o_ref, acc_sc, *, tq, tk, lambda_init):
    i = pl.program_id(1)
    qt = qt_ref[0]
    drow = lax.broadcasted_iota(jnp.int32, qt.shape, 0)
    zero = jnp.zeros_like(qt)
    qm = (jnp.where(drow < D_HEAD, qt, zero), jnp.where(drow >= D_HEAD, qt, zero))
    acc_sc[...] = jnp.zeros(acc_sc.shape, F32)

    def step(j, m, masked):
        k = k_ref[0, pl.ds(pl.multiple_of(j * tk, tk), tk), :]
        vt = vt_ref[0, j]
        if masked:
            key = lax.broadcasted_iota(jnp.int32, (tk, tq), 0) + j * tk
            qry = lax.broadcasted_iota(jnp.int32, (tk, tq), 1) + i * tq
            keep = key <= qry
        m_out = []
        scores = [jnp.dot(k, qm[mp], preferred_element_type=F32) for mp in range(2)]
        for mp in range(2):
            s = scores[mp]
            if masked:
                s = jnp.where(keep, s, -jnp.inf)
            m_new = jnp.maximum(m[mp], jnp.max(s, axis=0, keepdims=True))
            p = jnp.exp2((s - m_new).astype(BF16))
            alpha = jnp.exp2(m[mp] - m_new)
            acc_sc[mp] = alpha * acc_sc[mp] + jnp.dot(vt, p, preferred_element_type=F32)
            m_out.append(m_new)
        return tuple(m_out)

    n_sub = tq // tk
    neg = jnp.full((1, tq), -jnp.inf, F32)
    m = lax.fori_loop(0, i * n_sub, lambda j, c: step(j, c, False), (neg, neg))
    for d in range(n_sub):
        m = step(i * n_sub + d, m, True)

    lam = (jnp.exp(jnp.sum(lq1_ref[...] * lk1_ref[...], axis=1, keepdims=True))
           - jnp.exp(jnp.sum(lq2_ref[...] * lk2_ref[...], axis=1, keepdims=True))
           + lambda_init)
    a0, a1 = acc_sc[0], acc_sc[1]
    ot = (a0[:D_VALUE] * (1.0 / a0[D_VALUE:D_VALUE + 1])
          - lam * (a1[:D_VALUE] * (1.0 / a1[D_VALUE:D_VALUE + 1])))
    o = ot.T
    o = o * lax.rsqrt(jnp.mean(o * o, axis=1, keepdims=True) + NORM_EPS)
    o = o * gain_ref[...] * (1.0 - lambda_init)
    z = z_ref[0].astype(F32)
    o_ref[...] = (o * (z * _sigmoid(z))).astype(o_ref.dtype)


def _attention(proj, q_t, v_t, lq1, lk1, lq2, lk2, gain, n_heads, lambda_init, tq, tk):
    n_slabs, s_len, _ = proj.shape
    kern = functools.partial(_attn_kernel, tq=tq, tk=tk, lambda_init=lambda_init)
    vec = lambda n: pl.BlockSpec((1, n), lambda h, i: (0, 0))
    return pl.pallas_call(
        kern,
        out_shape=jax.ShapeDtypeStruct((s_len, n_heads * D_VALUE), BF16),
        grid=(n_heads, s_len // tq),
        in_specs=[vec(D_HEAD), vec(D_HEAD), vec(D_HEAD), vec(D_HEAD), vec(D_VALUE),
                  pl.BlockSpec((1, LANES, tq), lambda h, i: (h, 0, i)),
                  pl.BlockSpec((1, s_len, LANES), lambda h, i: (n_heads + h, 0, 0)),
                  pl.BlockSpec((1, s_len // tk, D_VALUE + ONES_ROWS, tk),
                               lambda h, i: (h, 0, 0, 0)),
                  pl.BlockSpec((1, tq, LANES), lambda h, i: (3 * n_heads + h, i, 0))],
        out_specs=pl.BlockSpec((tq, D_VALUE), lambda h, i: (i, h)),
        scratch_shapes=[pltpu.VMEM((2, D_VALUE + ONES_ROWS, tq), F32)],
        compiler_params=pltpu.CompilerParams(
            dimension_semantics=("arbitrary", "arbitrary"), vmem_limit_bytes=VMEM_LIMIT_BYTES),
        name="attn",
    )(lq1, lk1, lq2, lk2, gain, q_t, proj, v_t, proj)


def _out_kernel(ya_ref, cb_ref, cc_ref, cx_ref, zb_ref, hc_ref, hx_ref, ga_ref, gb_ref, x_ref,
                gate_ref, cw_ref, pa_ref, pb_ref, wo_ref, lng_ref, lnb_ref, o_ref,
                *, tm, conv_slabs, gate_slabs, alpha):
    i = pl.program_id(0)
    row = lax.broadcasted_iota(jnp.int32, (tm, LANES), 0)
    have_prev = (i > 0).astype(F32)
    yb = []
    for s in range(conv_slabs):
        u = cc_ref[s].astype(F32) * cx_ref[s].astype(F32)
        prev = hc_ref[s].astype(F32) * hx_ref[s].astype(F32) * have_prev
        p1 = prev[SUBLANES - 1:SUBLANES, :]
        p2 = prev[SUBLANES - 2:SUBLANES - 1, :]
        u1 = jnp.where(row == 0, p1, pltpu.roll(u, 1, axis=0))
        u2 = jnp.where(row == 0, p2, jnp.where(row == 1, p1, pltpu.roll(u, 2, axis=0)))
        w = cw_ref[:, s * LANES:(s + 1) * LANES]
        conv = w[0:1, :] * u2 + w[1:2, :] * u1 + w[2:3, :] * u
        z = zb_ref[s].astype(F32)
        yb.append((cb_ref[s].astype(F32) * conv * (z * _sigmoid(z))).astype(BF16))
    y_b = jnp.concatenate(yb, axis=1)
    ha = jnp.dot(ya_ref[...], pa_ref[...], preferred_element_type=F32)
    hb = jnp.dot(y_b, pb_ref[...], preferred_element_type=F32)
    g_a = jnp.concatenate([ga_ref[s] for s in range(gate_slabs)], axis=1).astype(F32)
    g_b = jnp.concatenate([gb_ref[s] for s in range(gate_slabs)], axis=1).astype(F32)
    h = (_sigmoid(g_a) * ha + _sigmoid(g_b) * hb).astype(BF16)
    out = jnp.dot(h, wo_ref[...], preferred_element_type=F32)
    r = alpha * x_ref[...] + gate_ref[...] * out
    mu = jnp.mean(r, axis=1, keepdims=True)
    rc = r - mu
    var = jnp.mean(rc * rc, axis=1, keepdims=True)
    o_ref[...] = rc * lax.rsqrt(var + NORM_EPS) * lng_ref[...] + lnb_ref[...]


def _out_stage(y_a, proj, x2, gate, conv_w, pa_bf, pb_bf, wo_bf, ln_g, ln_b, alpha,
               seg_off, tm=256):
    s_len, d = x2.shape
    conv_w_width = conv_w.shape[1]
    conv_slabs = conv_w_width // LANES
    gate_slabs = d // LANES
    off_cb, off_cc, off_cx, off_zb, off_ga, off_gb = [o // LANES for o in seg_off]
    kern = functools.partial(_out_kernel, tm=tm, conv_slabs=conv_slabs, gate_slabs=gate_slabs,
                             alpha=alpha)
    slab = lambda nsl, off: pl.BlockSpec((nsl, tm, LANES), lambda i: (off // nsl, i, 0))
    halo = lambda off: pl.BlockSpec(
        (conv_slabs, SUBLANES, LANES),
        lambda i: (off // conv_slabs, jnp.maximum(i * (tm // SUBLANES) - 1, 0), 0))
    const = lambda shape: pl.BlockSpec(shape, lambda i: (0, 0), pipeline_mode=pl.Buffered(1))
    return pl.pallas_call(
        kern,
        out_shape=jax.ShapeDtypeStruct((s_len, d), F32),
        grid=(s_len // tm,),
        in_specs=[pl.BlockSpec((tm, y_a.shape[1]), lambda i: (i, 0)),
                  slab(conv_slabs, off_cb), slab(conv_slabs, off_cc), slab(conv_slabs, off_cx),
                  slab(conv_slabs, off_zb), halo(off_cc), halo(off_cx),
                  slab(gate_slabs, off_ga), slab(gate_slabs, off_gb),
                  pl.BlockSpec((tm, d), lambda i: (i, 0)),
                  const((1, d)), const((CONV_SIZE, conv_w_width)),
                  const(pa_bf.shape), const(pb_bf.shape), const(wo_bf.shape),
                  const((1, d)), const((1, d))],
        out_specs=pl.BlockSpec((tm, d), lambda i: (i, 0)),
        compiler_params=pltpu.CompilerParams(
            dimension_semantics=("arbitrary",), vmem_limit_bytes=VMEM_LIMIT_BYTES),
        name="out_stage",
    )(y_a, proj, proj, proj, proj, proj, proj, proj, proj, x2, gate, conv_w, pa_bf, pb_bf, wo_bf,
      ln_g, ln_b)


def _rope_tables(positions_1d):
    inv_freq = ROPE_THETA ** (-jnp.arange(0, D_HEAD, 2, dtype=F32) / D_HEAD)
    ang = positions_1d.astype(F32)[:, None] * inv_freq
    cos, sin = jnp.cos(ang), jnp.sin(ang)
    reps = LANES // D_HEAD
    cos_t = jnp.tile(jnp.concatenate([cos, cos], axis=1), (1, reps))
    sin_t = jnp.tile(jnp.concatenate([-sin, sin], axis=1), (1, reps))
    return cos_t, sin_t


def kernel(x, c, positions, w_mod, b_mod, w_in, lambda_q1, lambda_k1, lambda_q2, lambda_k2,
           subln_gain, conv_w, w_proj_a, w_proj_b, w_out, ln_gain, ln_bias):
    batch, s_len, d = x.shape
    depth = w_mod.shape[0]
    attn_width = w_proj_a.shape[1]
    conv_width = w_proj_b.shape[1]
    n_heads = attn_width // D_VALUE
    alpha = (2.0 * depth) ** 0.25
    seg_off = (4 * attn_width, 4 * attn_width + conv_width, 4 * attn_width + 2 * conv_width,
               4 * attn_width + 3 * conv_width, 4 * attn_width + 4 * conv_width,
               4 * attn_width + 4 * conv_width + d)
    outs = []
    for b in range(batch):
        h = x[b]
        cos_t, sin_t = _rope_tables(positions[b])
        c_col = c[b].reshape(d, 1)
        for l in range(depth):
            mod = _mod(c_col, w_mod[l], b_mod[l].reshape(1, 3 * d))
            shift, scale, gate = mod[:, :d], mod[:, d:2 * d], mod[:, 2 * d:]
            proj, q_t, v_t = _in_proj(h, scale, shift, cos_t, sin_t, w_in[l].astype(BF16),
                                      attn_width, ATTN_TK)
            y_a = _attention(proj, q_t, v_t,
                             lambda_q1[l].reshape(1, D_HEAD), lambda_k1[l].reshape(1, D_HEAD),
                             lambda_q2[l].reshape(1, D_HEAD), lambda_k2[l].reshape(1, D_HEAD),
                             subln_gain[l].reshape(1, D_VALUE), n_heads, _lambda_init(l),
                             ATTN_TQ, ATTN_TK)
            h = _out_stage(y_a, proj, h, gate, conv_w[l], w_proj_a[l].astype(BF16),
                           w_proj_b[l].astype(BF16), w_out[l].astype(BF16),
                           ln_gain[l].reshape(1, d), ln_bias[l].reshape(1, d), alpha, seg_off)
        outs.append(h)
    return outs[0][None] if batch == 1 else jnp.stack(outs, axis=0)
```

```python
import functools
import math

import jax
import jax.numpy as jnp
from jax import lax
from jax.experimental import pallas as pl
from jax.experimental.pallas import tpu as pltpu

D_HEAD = 64
D_VALUE = 2 * D_HEAD
CONV_SIZE = 3
ROPE_THETA = 10000.0
NORM_EPS = 1e-5

LANES = 128
SUBLANES = 8
MXU_N = 256
VMEM_LIMIT_BYTES = 56 * 1024 * 1024
ATTN_TQ = 512
ATTN_TK = 512
ONES_ROWS = 16
LOG2_E = math.log2(math.e)

F32 = jnp.float32
BF16 = jnp.bfloat16


def _lambda_init(layer):
    return 0.8 - 0.6 * math.exp(-0.3 * layer)


def _sigmoid(z):
    return 1.0 / (1.0 + jnp.exp(-z))


def _mod_kernel(c_ref, w_ref, b_ref, o_ref):
    o_ref[...] = jnp.sum(w_ref[...] * c_ref[...], axis=0, keepdims=True) + b_ref[...]


def _mod(c_col, w_mod, b_mod, tn=512):
    d, n = w_mod.shape
    return pl.pallas_call(
        _mod_kernel,
        out_shape=jax.ShapeDtypeStruct((1, n), F32),
        grid=(n // tn,),
        in_specs=[pl.BlockSpec((d, 1), lambda j: (0, 0)),
                  pl.BlockSpec((d, tn), lambda j: (0, j)),
                  pl.BlockSpec((1, tn), lambda j: (0, j))],
        out_specs=pl.BlockSpec((1, tn), lambda j: (0, j)),
        compiler_params=pltpu.CompilerParams(
            dimension_semantics=("arbitrary",), vmem_limit_bytes=VMEM_LIMIT_BYTES),
        name="mod",
    )(c_col, w_mod, b_mod)


def _inproj_kernel(x_ref, scale_ref, shift_ref, cos_ref, sin_ref, w_ref, o_ref, qt_ref, vt_ref, u_sc,
                   *, slabs, tk):
    j = pl.program_id(1)
    tm = x_ref.shape[0]

    @pl.when(j == 0)
    def _():
        u_sc[...] = (x_ref[...] * (1.0 + scale_ref[...]) + shift_ref[...]).astype(BF16)

    acc = jnp.dot(u_sc[...], w_ref[...], preferred_element_type=F32)

    def rope(t):
        lane = lax.broadcasted_iota(jnp.int32, t.shape, 1)
        first_half = (lane % D_HEAD) < (D_HEAD // 2)
        partner = jnp.where(first_half,
                            pltpu.roll(t, LANES - D_HEAD // 2, axis=1),
                            pltpu.roll(t, D_HEAD // 2, axis=1))
        return t * cos_ref[...] + partner * sin_ref[...]

    @pl.when(j == 0)
    def _():
        for s in range(slabs):
            r = rope(acc[:, s * LANES:(s + 1) * LANES]) * (D_HEAD ** -0.5 * LOG2_E)
            o_ref[s] = r.astype(BF16)
            qt_ref[s] = r.T.astype(BF16)

    @pl.when(j == 1)
    def _():
        for s in range(slabs):
            o_ref[s] = rope(acc[:, s * LANES:(s + 1) * LANES]).astype(BF16)

    @pl.when(j == 2)
    def _():
        for s in range(slabs):
            t = acc[:, s * LANES:(s + 1) * LANES]
            o_ref[s] = t.astype(BF16)
            for cblk in range(tm // tk):
                vt_ref[s, cblk, :LANES, :] = t[cblk * tk:(cblk + 1) * tk, :].T.astype(BF16)
                vt_ref[s, cblk, LANES:, :] = jnp.ones((ONES_ROWS, tk), BF16)

    @pl.when(j >= 3)
    def _():
        for s in range(slabs):
            o_ref[s] = acc[:, s * LANES:(s + 1) * LANES].astype(BF16)


def _in_proj(x2, scale, shift, cos_t, sin_t, w_in_bf, attn_width, tk, tm=1024):
    s_len, d = x2.shape
    n = w_in_bf.shape[1]
    tn = attn_width
    slabs = tn // LANES
    kern = functools.partial(_inproj_kernel, slabs=slabs, tk=tk)
    return pl.pallas_call(
        kern,
        out_shape=(jax.ShapeDtypeStruct((n // LANES, s_len, LANES), BF16),
                   jax.ShapeDtypeStruct((slabs, LANES, s_len), BF16),
                   jax.ShapeDtypeStruct((slabs, s_len // tk, LANES + ONES_ROWS, tk), BF16)),
        grid=(s_len // tm, n // tn),
        in_specs=[pl.BlockSpec((tm, d), lambda i, j: (i, 0)),
                  pl.BlockSpec((1, d), lambda i, j: (0, 0)),
                  pl.BlockSpec((1, d), lambda i, j: (0, 0)),
                  pl.BlockSpec((tm, LANES), lambda i, j: (i, 0)),
                  pl.BlockSpec((tm, LANES), lambda i, j: (i, 0)),
                  pl.BlockSpec((d, tn), lambda i, j: (0, j))],
        out_specs=(pl.BlockSpec((slabs, tm, LANES), lambda i, j: (j, i, 0)),
                   pl.BlockSpec((slabs, LANES, tm), lambda i, j: (0, 0, i)),
                   pl.BlockSpec((slabs, tm // tk, LANES + ONES_ROWS, tk), lambda i, j: (0, i, 0, 0))),
        scratch_shapes=[pltpu.VMEM((tm, d), BF16)],
        compiler_params=pltpu.CompilerParams(
            dimension_semantics=("arbitrary", "arbitrary"), vmem_limit_bytes=VMEM_LIMIT_BYTES),
        name="in_proj",
    )(x2, scale, shift, cos_t, sin_t, w_in_bf)


def _attn_kernel(lq1_ref, lk1_ref, lq2_ref, lk2_ref, gain_ref, qt_ref, k_ref, vt_ref, z_ref,
                 o_ref, acc_sc, s_sc, mx_sc, qm_sc, *, tq, tk, lambda_init):
    i = pl.program_id(1)
    qt = qt_ref[0]
    drow = lax.broadcasted_iota(jnp.int32, qt.shape, 0)
    zero = jnp.zeros_like(qt)
    qm_sc[0] = jnp.where(drow < D_HEAD, qt, zero)
    qm_sc[1] = jnp.where(drow >= D_HEAD, qt, zero)
    acc_sc[...] = jnp.zeros(acc_sc.shape, F32)
    units = [(mp, c) for mp in range(2) for c in range(tq // MXU_N)]
    cols = lambda c: slice(c * MXU_N, (c + 1) * MXU_N)

    def qk_unit(j, slot, mp, c):
        k = k_ref[0, pl.ds(pl.multiple_of(j * tk, tk), tk), :]
        s = jnp.dot(k, qm_sc[mp, :, cols(c)], preferred_element_type=F32)
        s_sc[slot, mp, :, cols(c)] = s
        mx_sc[slot, mp, :, cols(c)] = jnp.max(s, axis=0, keepdims=True)

    def softmax_pv_unit(j, slot, mp, c, m_old, masked):
        s = s_sc[slot, mp, :, cols(c)]
        if masked:
            key = lax.broadcasted_iota(jnp.int32, s.shape, 0)
            qry = lax.broadcasted_iota(jnp.int32, s.shape, 1) + c * MXU_N
            s = jnp.where(key <= qry, s, -jnp.inf)
            m_cur = jnp.max(s, axis=0, keepdims=True)
        else:
            m_cur = mx_sc[slot, mp, :, cols(c)]
        m_new = jnp.maximum(m_old, m_cur)
        p = jnp.exp2((s - m_new).astype(BF16))
        alpha = jnp.exp2(m_old - m_new)
        acc_sc[mp, :, cols(c)] = alpha * acc_sc[mp, :, cols(c)] + jnp.dot(
            vt_ref[0, j], p, preferred_element_type=F32)
        return m_new

    def stage(j_next, slot_next, j, slot, m, masked):
        m_out = []
        for u, (mp, c) in enumerate(units):
            if j_next is not None:
                qk_unit(j_next, slot_next, mp, c)
            m_out.append(softmax_pv_unit(j, slot, mp, c, m[u], masked))
        return tuple(m_out)

    def pair(jj, m):
        j = 2 * jj
        m = stage(j + 1, 1, j, 0, m, False)
        return stage(j + 2, 0, j + 1, 1, m, False)

    neg = jnp.full((1, MXU_N), -jnp.inf, F32)
    for mp, c in units:
        qk_unit(0, 0, mp, c)
    m = lax.fori_loop(0, i // 4, lambda q, m: pair(2 * q + 1, pair(2 * q, m)), (neg,) * len(units))
    m = lax.fori_loop(2 * (i // 4), i // 2, pair, m)

    @pl.when(i % 2 == 1)
    def _():
        m1 = stage(i, 1, i - 1, 0, m, False)
        stage(None, None, i, 1, m1, True)

    @pl.when(i % 2 == 0)
    def _():
        stage(None, None, i, 0, m, True)

    lam = (jnp.exp(jnp.sum(lq1_ref[...] * lk1_ref[...], axis=1, keepdims=True))
           - jnp.exp(jnp.sum(lq2_ref[...] * lk2_ref[...], axis=1, keepdims=True))
           + lambda_init)
    a0, a1 = acc_sc[0], acc_sc[1]
    ot = (a0[:D_VALUE] * (1.0 / a0[D_VALUE:D_VALUE + 1])
          - lam * (a1[:D_VALUE] * (1.0 / a1[D_VALUE:D_VALUE + 1])))
    o = ot.T
    o = o * lax.rsqrt(jnp.mean(o * o, axis=1, keepdims=True) + NORM_EPS)
    o = o * gain_ref[...] * (1.0 - lambda_init)
    z = z_ref[0].astype(F32)
    o_ref[...] = (o * (z * _sigmoid(z))).astype(o_ref.dtype)


def _attention(proj, q_t, v_t, lq1, lk1, lq2, lk2, gain, n_heads, lambda_init, tq, tk):
    n_slabs, s_len, _ = proj.shape
    assert tq == tk, "the diagonal tile mask assumes square tiles"
    kern = functools.partial(_attn_kernel, tq=tq, tk=tk, lambda_init=lambda_init)
    vec = lambda n: pl.BlockSpec((1, n), lambda h, i: (0, 0))
    return pl.pallas_call(
        kern,
        out_shape=jax.ShapeDtypeStruct((s_len, n_heads * D_VALUE), BF16),
        grid=(n_heads, s_len // tq),
        in_specs=[vec(D_HEAD), vec(D_HEAD), vec(D_HEAD), vec(D_HEAD), vec(D_VALUE),
                  pl.BlockSpec((1, LANES, tq), lambda h, i: (h, 0, i)),
                  pl.BlockSpec((1, s_len, LANES), lambda h, i: (n_heads + h, 0, 0)),
                  pl.BlockSpec((1, s_len // tk, D_VALUE + ONES_ROWS, tk),
                               lambda h, i: (h, 0, 0, 0)),
                  pl.BlockSpec((1, tq, LANES), lambda h, i: (3 * n_heads + h, i, 0))],
        out_specs=pl.BlockSpec((tq, D_VALUE), lambda h, i: (i, h)),
        scratch_shapes=[pltpu.VMEM((2, D_VALUE + ONES_ROWS, tq), F32),
                        pltpu.VMEM((2, 2, tk, tq), F32),
                        pltpu.VMEM((2, 2, 1, tq), F32),
                        pltpu.VMEM((2, LANES, tq), BF16)],
        compiler_params=pltpu.CompilerParams(
            dimension_semantics=("arbitrary", "arbitrary"), vmem_limit_bytes=VMEM_LIMIT_BYTES,
            ),
        name="attn",
    )(lq1, lk1, lq2, lk2, gain, q_t, proj, v_t, proj)


def _out_kernel(ya_ref, cb_ref, cc_ref, cx_ref, zb_ref, hc_ref, hx_ref, ga_ref, gb_ref, x_ref,
                gate_ref, cw_ref, pa_ref, pb_ref, wo_ref, lng_ref, lnb_ref, o_ref,
                *, tm, conv_slabs, gate_slabs, alpha):
    i = pl.program_id(0)
    row = lax.broadcasted_iota(jnp.int32, (tm, LANES), 0)
    have_prev = (i > 0).astype(F32)
    yb = []
    for s in range(conv_slabs):
        u = cc_ref[s].astype(F32) * cx_ref[s].astype(F32)
        prev = hc_ref[s].astype(F32) * hx_ref[s].astype(F32) * have_prev
        p1 = prev[SUBLANES - 1:SUBLANES, :]
        p2 = prev[SUBLANES - 2:SUBLANES - 1, :]
        u1 = jnp.where(row == 0, p1, pltpu.roll(u, 1, axis=0))
        u2 = jnp.where(row == 0, p2, jnp.where(row == 1, p1, pltpu.roll(u, 2, axis=0)))
        w = cw_ref[:, s * LANES:(s + 1) * LANES]
        conv = w[0:1, :] * u2 + w[1:2, :] * u1 + w[2:3, :] * u
        z = zb_ref[s].astype(F32)
        yb.append((cb_ref[s].astype(F32) * conv * (z * _sigmoid(z))).astype(BF16))
    y_b = jnp.concatenate(yb, axis=1)
    ha = jnp.dot(ya_ref[...], pa_ref[...], preferred_element_type=F32)
    hb = jnp.dot(y_b, pb_ref[...], preferred_element_type=F32)
    g_a = jnp.concatenate([ga_ref[s] for s in range(gate_slabs)], axis=1).astype(F32)
    g_b = jnp.concatenate([gb_ref[s] for s in range(gate_slabs)], axis=1).astype(F32)
    h = (_sigmoid(g_a) * ha + _sigmoid(g_b) * hb).astype(BF16)
    out = jnp.dot(h, wo_ref[...], preferred_element_type=F32)
    r = alpha * x_ref[...] + gate_ref[...] * out
    mu = jnp.mean(r, axis=1, keepdims=True)
    rc = r - mu
    var = jnp.mean(rc * rc, axis=1, keepdims=True)
    o_ref[...] = rc * lax.rsqrt(var + NORM_EPS) * lng_ref[...] + lnb_ref[...]


def _out_stage(y_a, proj, x2, gate, conv_w, pa_bf, pb_bf, wo_bf, ln_g, ln_b, alpha,
               seg_off, tm=256):
    s_len, d = x2.shape
    conv_w_width = conv_w.shape[1]
    conv_slabs = conv_w_width // LANES
    gate_slabs = d // LANES
    off_cb, off_cc, off_cx, off_zb, off_ga, off_gb = [o // LANES for o in seg_off]
    kern = functools.partial(_out_kernel, tm=tm, conv_slabs=conv_slabs, gate_slabs=gate_slabs,
                             alpha=alpha)
    slab = lambda nsl, off: pl.BlockSpec((nsl, tm, LANES), lambda i: (off // nsl, i, 0))
    halo = lambda off: pl.BlockSpec(
        (conv_slabs, SUBLANES, LANES),
        lambda i: (off // conv_slabs, jnp.maximum(i * (tm // SUBLANES) - 1, 0), 0))
    const = lambda shape: pl.BlockSpec(shape, lambda i: (0, 0), pipeline_mode=pl.Buffered(1))
    return pl.pallas_call(
        kern,
        out_shape=jax.ShapeDtypeStruct((s_len, d), F32),
        grid=(s_len // tm,),
        in_specs=[pl.BlockSpec((tm, y_a.shape[1]), lambda i: (i, 0)),
                  slab(conv_slabs, off_cb), slab(conv_slabs, off_cc), slab(conv_slabs, off_cx),
                  slab(conv_slabs, off_zb), halo(off_cc), halo(off_cx),
                  slab(gate_slabs, off_ga), slab(gate_slabs, off_gb),
                  pl.BlockSpec((tm, d), lambda i: (i, 0)),
                  const((1, d)), const((CONV_SIZE, conv_w_width)),
                  const(pa_bf.shape), const(pb_bf.shape), const(wo_bf.shape),
                  const((1, d)), const((1, d))],
        out_specs=pl.BlockSpec((tm, d), lambda i: (i, 0)),
        compiler_params=pltpu.CompilerParams(
            dimension_semantics=("arbitrary",), vmem_limit_bytes=VMEM_LIMIT_BYTES),
        name="out_stage",
    )(y_a, proj, proj, proj, proj, proj, proj, proj, proj, x2, gate, conv_w, pa_bf, pb_bf, wo_bf,
      ln_g, ln_b)


def _rope_tables(positions_1d):
    inv_freq = ROPE_THETA ** (-jnp.arange(0, D_HEAD, 2, dtype=F32) / D_HEAD)
    ang = positions_1d.astype(F32)[:, None] * inv_freq
    cos, sin = jnp.cos(ang), jnp.sin(ang)
    reps = LANES // D_HEAD
    cos_t = jnp.tile(jnp.concatenate([cos, cos], axis=1), (1, reps))
    sin_t = jnp.tile(jnp.concatenate([-sin, sin], axis=1), (1, reps))
    return cos_t, sin_t


def kernel(x, c, positions, w_mod, b_mod, w_in, lambda_q1, lambda_k1, lambda_q2, lambda_k2,
           subln_gain, conv_w, w_proj_a, w_proj_b, w_out, ln_gain, ln_bias):
    batch, s_len, d = x.shape
    depth = w_mod.shape[0]
    attn_width = w_proj_a.shape[1]
    conv_width = w_proj_b.shape[1]
    n_heads = attn_width // D_VALUE
    alpha = (2.0 * depth) ** 0.25
    seg_off = (4 * attn_width, 4 * attn_width + conv_width, 4 * attn_width + 2 * conv_width,
               4 * attn_width + 3 * conv_width, 4 * attn_width + 4 * conv_width,
               4 * attn_width + 4 * conv_width + d)
    outs = []
    for b in range(batch):
        h = x[b]
        cos_t, sin_t = _rope_tables(positions[b])
        c_col = c[b].reshape(d, 1)
        for l in range(depth):
            mod = _mod(c_col, w_mod[l], b_mod[l].reshape(1, 3 * d))
            shift, scale, gate = mod[:, :d], mod[:, d:2 * d], mod[:, 2 * d:]
            proj, q_t, v_t = _in_proj(h, scale, shift, cos_t, sin_t, w_in[l].astype(BF16),
                                      attn_width, ATTN_TK)
            y_a = _attention(proj, q_t, v_t,
                             lambda_q1[l].reshape(1, D_HEAD), lambda_k1[l].reshape(1, D_HEAD),
                             lambda_q2[l].reshape(1, D_HEAD), lambda_k2[l].reshape(1, D_HEAD),
                             subln_gain[l].reshape(1, D_VALUE), n_heads, _lambda_init(l),
                             ATTN_TQ, ATTN_TK)
            h = _out_stage(y_a, proj, h, gate, conv_w[l], w_proj_a[l].astype(BF16),
                           w_proj_b[l].astype(BF16), w_out[l].astype(BF16),
                           ln_gain[l].reshape(1, d), ln_bias[l].reshape(1, d), alpha, seg_off)
        outs.append(h)
    return outs[0][None] if batch == 1 else jnp.stack(outs, axis=0)
```

```python
import functools
import math

import jax
import jax.numpy as jnp
from jax import lax
from jax.experimental import pallas as pl
from jax.experimental.pallas import tpu as pltpu

D_HEAD = 64
D_VALUE = 2 * D_HEAD
CONV_SIZE = 3
ROPE_THETA = 10000.0
NORM_EPS = 1e-5

LANES = 128
SUBLANES = 8
MXU_N = 256
VMEM_LIMIT_BYTES = 56 * 1024 * 1024
ATTN_TQ = 512
ATTN_TK = 512
ONES_ROWS = 16
LOG2_E = math.log2(math.e)

F32 = jnp.float32
BF16 = jnp.bfloat16


def _lambda_init(layer):
    return 0.8 - 0.6 * math.exp(-0.3 * layer)


def _sigmoid(z):
    return 1.0 / (1.0 + jnp.exp(-z))


def _mod_kernel(c_ref, w_ref, b_ref, o_ref):
    o_ref[...] = jnp.sum(w_ref[...] * c_ref[...], axis=0, keepdims=True) + b_ref[...]


def _mod(c_col, w_mod, b_mod, tn=512):
    d, n = w_mod.shape
    return pl.pallas_call(
        _mod_kernel,
        out_shape=jax.ShapeDtypeStruct((1, n), F32),
        grid=(n // tn,),
        in_specs=[pl.BlockSpec((d, 1), lambda j: (0, 0)),
                  pl.BlockSpec((d, tn), lambda j: (0, j)),
                  pl.BlockSpec((1, tn), lambda j: (0, j))],
        out_specs=pl.BlockSpec((1, tn), lambda j: (0, j)),
        compiler_params=pltpu.CompilerParams(
            dimension_semantics=("arbitrary",), vmem_limit_bytes=VMEM_LIMIT_BYTES),
        name="mod",
    )(c_col, w_mod, b_mod)


def _inproj_kernel(x_ref, scale_ref, shift_ref, cos_ref, sin_ref, w_ref, o_ref, qt_ref, vt_ref, u_sc,
                   *, slabs, tk):
    j = pl.program_id(1)
    tm = x_ref.shape[0]

    @pl.when(j == 0)
    def _():
        u_sc[...] = (x_ref[...] * (1.0 + scale_ref[...]) + shift_ref[...]).astype(BF16)

    def project(epilogue):
        for c in range(slabs * LANES // MXU_N):
            acc = jnp.dot(u_sc[...], w_ref[:, c * MXU_N:(c + 1) * MXU_N],
                          preferred_element_type=F32)
            for h in range(MXU_N // LANES):
                o_ref[c * (MXU_N // LANES) + h] = epilogue(
                    acc[:, h * LANES:(h + 1) * LANES]).astype(BF16)

    @pl.when(j < 2)
    def _():
        qscale = jnp.where(j == 0, D_HEAD ** -0.5 * LOG2_E, 1.0).astype(F32)
        lane = lax.broadcasted_iota(jnp.int32, (tm, LANES), 1)
        first_half = (lane % D_HEAD) < (D_HEAD // 2)

        def rope(t):
            partner = jnp.where(first_half,
                                pltpu.roll(t, LANES - D_HEAD // 2, axis=1),
                                pltpu.roll(t, D_HEAD // 2, axis=1))
            return (t * cos_ref[...] + partner * sin_ref[...]) * qscale
        project(rope)

    @pl.when(j >= 2)
    def _():
        project(lambda t: t)

    @pl.when(j == 0)
    def _():
        for s in range(slabs):
            qt_ref[s] = o_ref[s].astype(F32).T.astype(BF16)

    @pl.when(j == 2)
    def _():
        for s in range(slabs):
            t = o_ref[s].astype(F32)
            for cblk in range(tm // tk):
                vt_ref[s, cblk, :LANES, :] = t[cblk * tk:(cblk + 1) * tk, :].T.astype(BF16)
                vt_ref[s, cblk, LANES:, :] = jnp.ones((ONES_ROWS, tk), BF16)


def _in_proj(x2, scale, shift, cos_t, sin_t, w_in_bf, attn_width, tk, tm=1024):
    s_len, d = x2.shape
    n = w_in_bf.shape[1]
    tn = attn_width
    slabs = tn // LANES
    kern = functools.partial(_inproj_kernel, slabs=slabs, tk=tk)
    return pl.pallas_call(
        kern,
        out_shape=(jax.ShapeDtypeStruct((n // LANES, s_len, LANES), BF16),
                   jax.ShapeDtypeStruct((slabs, LANES, s_len), BF16),
                   jax.ShapeDtypeStruct((slabs, s_len // tk, LANES + ONES_ROWS, tk), BF16)),
        grid=(s_len // tm, n // tn),
        in_specs=[pl.BlockSpec((tm, d), lambda i, j: (i, 0)),
                  pl.BlockSpec((1, d), lambda i, j: (0, 0)),
                  pl.BlockSpec((1, d), lambda i, j: (0, 0)),
                  pl.BlockSpec((tm, LANES), lambda i, j: (i, 0)),
                  pl.BlockSpec((tm, LANES), lambda i, j: (i, 0)),
                  pl.BlockSpec((d, tn), lambda i, j: (0, j))],
        out_specs=(pl.BlockSpec((slabs, tm, LANES), lambda i, j: (j, i, 0)),
                   pl.BlockSpec((slabs, LANES, tm), lambda i, j: (0, 0, i)),
                   pl.BlockSpec((slabs, tm // tk, LANES + ONES_ROWS, tk), lambda i, j: (0, i, 0, 0))),
        scratch_shapes=[pltpu.VMEM((tm, d), BF16)],
        compiler_params=pltpu.CompilerParams(
            dimension_semantics=("arbitrary", "arbitrary"), vmem_limit_bytes=VMEM_LIMIT_BYTES),
        name="in_proj",
    )(x2, scale, shift, cos_t, sin_t, w_in_bf)


def _attn_kernel(lq1_ref, lk1_ref, lq2_ref, lk2_ref, gain_ref, qt_ref, k_ref, vt_ref, z_ref,
                 o_ref, acc_sc, s_sc, mx_sc, qm_sc, *, tq, tk, lambda_init):
    i = pl.program_id(1)
    qt = qt_ref[0]
    drow = lax.broadcasted_iota(jnp.int32, qt.shape, 0)
    zero = jnp.zeros_like(qt)
    qm_sc[0] = jnp.where(drow < D_HEAD, qt, zero)
    qm_sc[1] = jnp.where(drow >= D_HEAD, qt, zero)
    acc_sc[...] = jnp.zeros(acc_sc.shape, F32)
    units = [(mp, c) for mp in range(2) for c in range(tq // MXU_N)]
    cols = lambda c: slice(c * MXU_N, (c + 1) * MXU_N)

    def qk_unit(j, slot, mp, c):
        k = k_ref[0, pl.ds(pl.multiple_of(j * tk, tk), tk), :]
        s = jnp.dot(k, qm_sc[mp, :, cols(c)], preferred_element_type=F32)
        s_sc[slot, mp, :, cols(c)] = s
        mx_sc[slot, mp, :, cols(c)] = jnp.max(s, axis=0, keepdims=True)

    def softmax_pv_unit(j, slot, mp, c, m_old, masked):
        s = s_sc[slot, mp, :, cols(c)]
        if masked:
            key = lax.broadcasted_iota(jnp.int32, s.shape, 0)
            qry = lax.broadcasted_iota(jnp.int32, s.shape, 1) + c * MXU_N
            s = jnp.where(key <= qry, s, -jnp.inf)
            m_cur = jnp.max(s, axis=0, keepdims=True)
        else:
            m_cur = mx_sc[slot, mp, :, cols(c)]
        m_new = jnp.maximum(m_old, m_cur)
        p = jnp.exp2((s - m_new).astype(BF16))
        alpha = jnp.exp2(m_old - m_new)
        acc_sc[mp, :, cols(c)] = alpha * acc_sc[mp, :, cols(c)] + jnp.dot(
            vt_ref[0, j], p, preferred_element_type=F32)
        return m_new

    def stage(j_next, slot_next, j, slot, m, masked):
        m_out = []
        for u, (mp, c) in enumerate(units):
            if j_next is not None:
                qk_unit(j_next, slot_next, mp, c)
            m_out.append(softmax_pv_unit(j, slot, mp, c, m[u], masked))
        return tuple(m_out)

    def pair(jj, m):
        j = 2 * jj
        m = stage(j + 1, 1, j, 0, m, False)
        return stage(j + 2, 0, j + 1, 1, m, False)

    neg = jnp.full((1, MXU_N), -jnp.inf, F32)
    for mp, c in units:
        qk_unit(0, 0, mp, c)
    quad = lambda q, m: pair(2 * q + 1, pair(2 * q, m))
    m = lax.fori_loop(0, i // 8, lambda o, m: quad(2 * o + 1, quad(2 * o, m)), (neg,) * len(units))
    m = lax.fori_loop(2 * (i // 8), i // 4, quad, m)
    m = lax.fori_loop(2 * (i // 4), i // 2, pair, m)

    @pl.when(i % 2 == 1)
    def _():
        m1 = stage(i, 1, i - 1, 0, m, False)
        stage(None, None, i, 1, m1, True)

    @pl.when(i % 2 == 0)
    def _():
        stage(None, None, i, 0, m, True)

    lam = (jnp.exp(jnp.sum(lq1_ref[...] * lk1_ref[...], axis=1, keepdims=True))
           - jnp.exp(jnp.sum(lq2_ref[...] * lk2_ref[...], axis=1, keepdims=True))
           + lambda_init)
    a0, a1 = acc_sc[0], acc_sc[1]
    ot = (a0[:D_VALUE] * (1.0 / a0[D_VALUE:D_VALUE + 1])
          - lam * (a1[:D_VALUE] * (1.0 / a1[D_VALUE:D_VALUE + 1])))
    o = ot.T
    o = o * lax.rsqrt(jnp.mean(o * o, axis=1, keepdims=True) + NORM_EPS)
    o = o * gain_ref[...] * (1.0 - lambda_init)
    z = z_ref[0].astype(F32)
    o_ref[...] = (o * (z * _sigmoid(z))).astype(o_ref.dtype)


def _attention(proj, q_t, v_t, lq1, lk1, lq2, lk2, gain, n_heads, lambda_init, tq, tk):
    n_slabs, s_len, _ = proj.shape
    assert tq == tk, "the diagonal tile mask assumes square tiles"
    kern = functools.partial(_attn_kernel, tq=tq, tk=tk, lambda_init=lambda_init)
    vec = lambda n: pl.BlockSpec((1, n), lambda h, i: (0, 0))
    return pl.pallas_call(
        kern,
        out_shape=jax.ShapeDtypeStruct((s_len, n_heads * D_VALUE), BF16),
        grid=(n_heads, s_len // tq),
        in_specs=[vec(D_HEAD), vec(D_HEAD), vec(D_HEAD), vec(D_HEAD), vec(D_VALUE),
                  pl.BlockSpec((1, LANES, tq), lambda h, i: (h, 0, i)),
                  pl.BlockSpec((1, s_len, LANES), lambda h, i: (n_heads + h, 0, 0)),
                  pl.BlockSpec((1, s_len // tk, D_VALUE + ONES_ROWS, tk),
                               lambda h, i: (h, 0, 0, 0)),
                  pl.BlockSpec((1, tq, LANES), lambda h, i: (3 * n_heads + h, i, 0))],
        out_specs=pl.BlockSpec((tq, D_VALUE), lambda h, i: (i, h)),
        scratch_shapes=[pltpu.VMEM((2, D_VALUE + ONES_ROWS, tq), F32),
                        pltpu.VMEM((2, 2, tk, tq), F32),
                        pltpu.VMEM((2, 2, 1, tq), F32),
                        pltpu.VMEM((2, LANES, tq), BF16)],
        compiler_params=pltpu.CompilerParams(
            dimension_semantics=("arbitrary", "arbitrary"), vmem_limit_bytes=VMEM_LIMIT_BYTES,
            ),
        name="attn",
    )(lq1, lk1, lq2, lk2, gain, q_t, proj, v_t, proj)


def _out_kernel(ya_ref, cb_ref, cc_ref, cx_ref, zb_ref, hc_ref, hx_ref, ga_ref, gb_ref, x_ref,
                gate_ref, cw_ref, pa_ref, pb_ref, wo_ref, lng_ref, lnb_ref, o_ref,
                *, tm, conv_slabs, gate_slabs, alpha):
    i = pl.program_id(0)
    row = lax.broadcasted_iota(jnp.int32, (tm, LANES), 0)
    have_prev = (i > 0).astype(F32)
    yb = []
    for s in range(conv_slabs):
        u = cc_ref[s].astype(F32) * cx_ref[s].astype(F32)
        prev = hc_ref[s].astype(F32) * hx_ref[s].astype(F32) * have_prev
        p1 = prev[SUBLANES - 1:SUBLANES, :]
        p2 = prev[SUBLANES - 2:SUBLANES - 1, :]
        u1 = jnp.where(row == 0, p1, pltpu.roll(u, 1, axis=0))
        u2 = jnp.where(row == 0, p2, jnp.where(row == 1, p1, pltpu.roll(u, 2, axis=0)))
        w = cw_ref[:, s * LANES:(s + 1) * LANES]
        conv = w[0:1, :] * u2 + w[1:2, :] * u1 + w[2:3, :] * u
        z = zb_ref[s].astype(F32)
        yb.append((cb_ref[s].astype(F32) * conv * (z * _sigmoid(z))).astype(BF16))
    y_b = jnp.concatenate(yb, axis=1)
    ha = jnp.dot(ya_ref[...], pa_ref[...], preferred_element_type=F32)
    hb = jnp.dot(y_b, pb_ref[...], preferred_element_type=F32)
    g_a = jnp.concatenate([ga_ref[s] for s in range(gate_slabs)], axis=1).astype(F32)
    g_b = jnp.concatenate([gb_ref[s] for s in range(gate_slabs)], axis=1).astype(F32)
    h = (_sigmoid(g_a) * ha + _sigmoid(g_b) * hb).astype(BF16)
    out = jnp.dot(h, wo_ref[...], preferred_element_type=F32)
    r = alpha * x_ref[...] + gate_ref[...] * out
    mu = jnp.mean(r, axis=1, keepdims=True)
    rc = r - mu
    var = jnp.mean(rc * rc, axis=1, keepdims=True)
    o_ref[...] = rc * lax.rsqrt(var + NORM_EPS) * lng_ref[...] + lnb_ref[...]


def _out_stage(y_a, proj, x2, gate, conv_w, pa_bf, pb_bf, wo_bf, ln_g, ln_b, alpha,
               seg_off, tm=256):
    s_len, d = x2.shape
    conv_w_width = conv_w.shape[1]
    conv_slabs = conv_w_width // LANES
    gate_slabs = d // LANES
    off_cb, off_cc, off_cx, off_zb, off_ga, off_gb = [o // LANES for o in seg_off]
    kern = functools.partial(_out_kernel, tm=tm, conv_slabs=conv_slabs, gate_slabs=gate_slabs,
                             alpha=alpha)
    slab = lambda nsl, off: pl.BlockSpec((nsl, tm, LANES), lambda i: (off // nsl, i, 0))
    halo = lambda off: pl.BlockSpec(
        (conv_slabs, SUBLANES, LANES),
        lambda i: (off // conv_slabs, jnp.maximum(i * (tm // SUBLANES) - 1, 0), 0))
    const = lambda shape: pl.BlockSpec(shape, lambda i: (0, 0), pipeline_mode=pl.Buffered(1))
    return pl.pallas_call(
        kern,
        out_shape=jax.ShapeDtypeStruct((s_len, d), F32),
        grid=(s_len // tm,),
        in_specs=[pl.BlockSpec((tm, y_a.shape[1]), lambda i: (i, 0)),
                  slab(conv_slabs, off_cb), slab(conv_slabs, off_cc), slab(conv_slabs, off_cx),
                  slab(conv_slabs, off_zb), halo(off_cc), halo(off_cx),
                  slab(gate_slabs, off_ga), slab(gate_slabs, off_gb),
                  pl.BlockSpec((tm, d), lambda i: (i, 0)),
                  const((1, d)), const((CONV_SIZE, conv_w_width)),
                  const(pa_bf.shape), const(pb_bf.shape), const(wo_bf.shape),
                  const((1, d)), const((1, d))],
        out_specs=pl.BlockSpec((tm, d), lambda i: (i, 0)),
        compiler_params=pltpu.CompilerParams(
            dimension_semantics=("arbitrary",), vmem_limit_bytes=VMEM_LIMIT_BYTES),
        name="out_stage",
    )(y_a, proj, proj, proj, proj, proj, proj, proj, proj, x2, gate, conv_w, pa_bf, pb_bf, wo_bf,
      ln_g, ln_b)


def _rope_tables(positions_1d):
    inv_freq = ROPE_THETA ** (-jnp.arange(0, D_HEAD, 2, dtype=F32) / D_HEAD)
    ang = positions_1d.astype(F32)[:, None] * inv_freq
    cos, sin = jnp.cos(ang), jnp.sin(ang)
    reps = LANES // D_HEAD
    cos_t = jnp.tile(jnp.concatenate([cos, cos], axis=1), (1, reps))
    sin_t = jnp.tile(jnp.concatenate([-sin, sin], axis=1), (1, reps))
    return cos_t, sin_t


def kernel(x, c, positions, w_mod, b_mod, w_in, lambda_q1, lambda_k1, lambda_q2, lambda_k2,
           subln_gain, conv_w, w_proj_a, w_proj_b, w_out, ln_gain, ln_bias):
    batch, s_len, d = x.shape
    depth = w_mod.shape[0]
    attn_width = w_proj_a.shape[1]
    conv_width = w_proj_b.shape[1]
    n_heads = attn_width // D_VALUE
    alpha = (2.0 * depth) ** 0.25
    seg_off = (4 * attn_width, 4 * attn_width + conv_width, 4 * attn_width + 2 * conv_width,
               4 * attn_width + 3 * conv_width, 4 * attn_width + 4 * conv_width,
               4 * attn_width + 4 * conv_width + d)
    outs = []
    for b in range(batch):
        h = x[b]
        cos_t, sin_t = _rope_tables(positions[b])
        c_col = c[b].reshape(d, 1)
        for l in range(depth):
            mod = _mod(c_col, w_mod[l], b_mod[l].reshape(1, 3 * d))
            shift, scale, gate = mod[:, :d], mod[:, d:2 * d], mod[:, 2 * d:]
            proj, q_t, v_t = _in_proj(h, scale, shift, cos_t, sin_t, w_in[l].astype(BF16),
                                      attn_width, ATTN_TK)
            y_a = _attention(proj, q_t, v_t,
                             lambda_q1[l].reshape(1, D_HEAD), lambda_k1[l].reshape(1, D_HEAD),
                             lambda_q2[l].reshape(1, D_HEAD), lambda_k2[l].reshape(1, D_HEAD),
                             subln_gain[l].reshape(1, D_VALUE), n_heads, _lambda_init(l),
                             ATTN_TQ, ATTN_TK)
            h = _out_stage(y_a, proj, h, gate, conv_w[l], w_proj_a[l].astype(BF16),
                           w_proj_b[l].astype(BF16), w_out[l].astype(BF16),
                           ln_gain[l].reshape(1, d), ln_bias[l].reshape(1, d), alpha, seg_off)
        outs.append(h)
    return outs[0][None] if batch == 1 else jnp.stack(outs, axis=0)
```

```python
import functools
import math

import jax
import jax.numpy as jnp
from jax import lax
from jax.experimental import pallas as pl
from jax.experimental.pallas import tpu as pltpu

D_HEAD = 64
D_VALUE = 2 * D_HEAD
CONV_SIZE = 3
ROPE_THETA = 10000.0
NORM_EPS = 1e-5

LANES = 128
SUBLANES = 8
MXU_N = 256
VMEM_LIMIT_BYTES = 56 * 1024 * 1024
ATTN_TQ = 512
ATTN_TK = 512
ONES_ROWS = 16
LOG2_E = math.log2(math.e)

F32 = jnp.float32
BF16 = jnp.bfloat16


def _lambda_init(layer):
    return 0.8 - 0.6 * math.exp(-0.3 * layer)


def _sigmoid(z):
    return 1.0 / (1.0 + jnp.exp(-z))


def _mod_kernel(c_ref, w_ref, b_ref, o_ref):
    o_ref[...] = jnp.sum(w_ref[...] * c_ref[...], axis=0, keepdims=True) + b_ref[...]


def _mod(c_col, w_mod, b_mod, tn=512):
    d, n = w_mod.shape
    return pl.pallas_call(
        _mod_kernel,
        out_shape=jax.ShapeDtypeStruct((1, n), F32),
        grid=(n // tn,),
        in_specs=[pl.BlockSpec((d, 1), lambda j: (0, 0)),
                  pl.BlockSpec((d, tn), lambda j: (0, j)),
                  pl.BlockSpec((1, tn), lambda j: (0, j))],
        out_specs=pl.BlockSpec((1, tn), lambda j: (0, j)),
        compiler_params=pltpu.CompilerParams(
            dimension_semantics=("arbitrary",), vmem_limit_bytes=VMEM_LIMIT_BYTES),
        name="mod",
    )(c_col, w_mod, b_mod)


def _inproj_kernel(x_ref, scale_ref, shift_ref, cos_ref, sin_ref, w_ref, o_ref, qt_ref, vt_ref, u_sc,
                   *, slabs, tk):
    j = pl.program_id(1)
    tm = x_ref.shape[0]

    @pl.when(j == 0)
    def _():
        u_sc[...] = (x_ref[...] * (1.0 + scale_ref[...]) + shift_ref[...]).astype(BF16)

    def project(epilogue):
        for c in range(slabs * LANES // MXU_N):
            acc = jnp.dot(u_sc[...], w_ref[:, c * MXU_N:(c + 1) * MXU_N],
                          preferred_element_type=F32)
            for h in range(MXU_N // LANES):
                o_ref[c * (MXU_N // LANES) + h] = epilogue(
                    acc[:, h * LANES:(h + 1) * LANES]).astype(BF16)

    @pl.when(j < 2)
    def _():
        qscale = jnp.where(j == 0, D_HEAD ** -0.5 * LOG2_E, 1.0).astype(F32)
        lane = lax.broadcasted_iota(jnp.int32, (tm, LANES), 1)
        first_half = (lane % D_HEAD) < (D_HEAD // 2)

        def rope(t):
            partner = jnp.where(first_half,
                                pltpu.roll(t, LANES - D_HEAD // 2, axis=1),
                                pltpu.roll(t, D_HEAD // 2, axis=1))
            return (t * cos_ref[...] + partner * sin_ref[...]) * qscale
        project(rope)

    @pl.when(j >= 2)
    def _():
        project(lambda t: t)

    @pl.when(j == 0)
    def _():
        for s in range(slabs):
            qt_ref[s] = o_ref[s].astype(F32).T.astype(BF16)

    @pl.when(j == 2)
    def _():
        for s in range(slabs):
            t = o_ref[s].astype(F32)
            for cblk in range(tm // tk):
                vt_ref[s, cblk, :LANES, :] = t[cblk * tk:(cblk + 1) * tk, :].T.astype(BF16)
                vt_ref[s, cblk, LANES:, :] = jnp.ones((ONES_ROWS, tk), BF16)


def _in_proj(x2, scale, shift, cos_t, sin_t, w_in_bf, attn_width, tk, tm=1024):
    s_len, d = x2.shape
    n = w_in_bf.shape[1]
    tn = attn_width
    slabs = tn // LANES
    kern = functools.partial(_inproj_kernel, slabs=slabs, tk=tk)
    return pl.pallas_call(
        kern,
        out_shape=(jax.ShapeDtypeStruct((n // LANES, s_len, LANES), BF16),
                   jax.ShapeDtypeStruct((slabs, LANES, s_len), BF16),
                   jax.ShapeDtypeStruct((slabs, s_len // tk, LANES + ONES_ROWS, tk), BF16)),
        grid=(s_len // tm, n // tn),
        in_specs=[pl.BlockSpec((tm, d), lambda i, j: (i, 0)),
                  pl.BlockSpec((1, d), lambda i, j: (0, 0)),
                  pl.BlockSpec((1, d), lambda i, j: (0, 0)),
                  pl.BlockSpec((tm, LANES), lambda i, j: (i, 0)),
                  pl.BlockSpec((tm, LANES), lambda i, j: (i, 0)),
                  pl.BlockSpec((d, tn), lambda i, j: (0, j))],
        out_specs=(pl.BlockSpec((slabs, tm, LANES), lambda i, j: (j, i, 0)),
                   pl.BlockSpec((slabs, LANES, tm), lambda i, j: (0, 0, i)),
                   pl.BlockSpec((slabs, tm // tk, LANES + ONES_ROWS, tk), lambda i, j: (0, i, 0, 0))),
        scratch_shapes=[pltpu.VMEM((tm, d), BF16)],
        compiler_params=pltpu.CompilerParams(
            dimension_semantics=("arbitrary", "arbitrary"), vmem_limit_bytes=VMEM_LIMIT_BYTES),
        name="in_proj",
    )(x2, scale, shift, cos_t, sin_t, w_in_bf)


def _attn_kernel(lq1_ref, lk1_ref, lq2_ref, lk2_ref, gain_ref, qt_ref, qtn_ref, k_ref, vt_ref, z_ref,
                 o_ref, acc_sc, s_sc, mx_sc, qm_sc, *, tq, tk, lambda_init):
    i = pl.program_id(1)
    drow = lax.broadcasted_iota(jnp.int32, (LANES, tq), 0)
    zero = jnp.zeros((LANES, tq), BF16)
    for which, ref in enumerate((qt_ref, qtn_ref)):
        qm_sc[which, 0] = jnp.where(drow < D_HEAD, ref[0], zero)
        qm_sc[which, 1] = jnp.where(drow >= D_HEAD, ref[0], zero)
    acc_sc[...] = jnp.zeros(acc_sc.shape, F32)
    units = [(mp, c) for mp in range(2) for c in range(tq // MXU_N)]
    cols = lambda c: slice(c * MXU_N, (c + 1) * MXU_N)

    def qk_unit(j, slot, mp, c, which=0):
        k = k_ref[0, pl.ds(pl.multiple_of(j * tk, tk), tk), :]
        s = jnp.dot(k, qm_sc[which, mp, :, cols(c)], preferred_element_type=F32)
        s_sc[slot, mp, :, cols(c)] = s
        mx_sc[slot, mp, :, cols(c)] = jnp.max(s, axis=0, keepdims=True)

    def softmax_pv_unit(j, slot, mp, c, m_old, masked):
        s = s_sc[slot, mp, :, cols(c)]
        if masked:
            key = lax.broadcasted_iota(jnp.int32, s.shape, 0)
            qry = lax.broadcasted_iota(jnp.int32, s.shape, 1) + c * MXU_N
            s = jnp.where(key <= qry, s, -jnp.inf)
            m_cur = jnp.max(s, axis=0, keepdims=True)
        else:
            m_cur = mx_sc[slot, mp, :, cols(c)]
        m_new = jnp.maximum(m_old, m_cur)
        p = jnp.exp2((s - m_new).astype(BF16))
        alpha = jnp.exp2(m_old - m_new)
        acc_sc[mp, :, cols(c)] = alpha * acc_sc[mp, :, cols(c)] + jnp.dot(
            vt_ref[0, j], p, preferred_element_type=F32)
        return m_new

    def stage(j_next, slot_next, j, slot, m, masked):
        m_out = []
        for u, (mp, c) in enumerate(units):
            if j_next is not None:
                qk_unit(j_next, slot_next, mp, c)
            m_out.append(softmax_pv_unit(j, slot, mp, c, m[u], masked))
        return tuple(m_out)

    def last_stage(slot, m):
        for u, (mp, c) in enumerate(units):
            softmax_pv_unit(i, slot, mp, c, m[u], True)
            qk_unit(0, 0, mp, c, which=1)

    def pair(jj, m):
        j = 2 * jj
        m = stage(j + 1, 1, j, 0, m, False)
        return stage(j + 2, 0, j + 1, 1, m, False)

    neg = jnp.full((1, MXU_N), -jnp.inf, F32)
    @pl.when(i == 0)
    def _():
        for mp, c in units:
            qk_unit(0, 0, mp, c)
    quad = lambda q, m: pair(2 * q + 1, pair(2 * q, m))
    m = lax.fori_loop(0, i // 8, lambda o, m: quad(2 * o + 1, quad(2 * o, m)), (neg,) * len(units))
    m = lax.fori_loop(2 * (i // 8), i // 4, quad, m)
    m = lax.fori_loop(2 * (i // 4), i // 2, pair, m)

    def finalize():
        lam = (jnp.exp(jnp.sum(lq1_ref[...] * lk1_ref[...], axis=1, keepdims=True))
               - jnp.exp(jnp.sum(lq2_ref[...] * lk2_ref[...], axis=1, keepdims=True))
               + lambda_init)
        a0, a1 = acc_sc[0], acc_sc[1]
        ot = (a0[:D_VALUE] * (1.0 / a0[D_VALUE:D_VALUE + 1])
              - lam * (a1[:D_VALUE] * (1.0 / a1[D_VALUE:D_VALUE + 1])))
        o = ot.T
        o = o * lax.rsqrt(jnp.mean(o * o, axis=1, keepdims=True) + NORM_EPS)
        o = o * gain_ref[...] * (1.0 - lambda_init)
        z = z_ref[0].astype(F32)
        o_ref[...] = (o * (z * _sigmoid(z))).astype(o_ref.dtype)

    @pl.when(i % 2 == 1)
    def _():
        last_stage(1, stage(i, 1, i - 1, 0, m, False))
        finalize()

    @pl.when(i % 2 == 0)
    def _():
        last_stage(0, m)
        finalize()


def _attention(proj, q_t, v_t, lq1, lk1, lq2, lk2, gain, n_heads, lambda_init, tq, tk):
    n_slabs, s_len, _ = proj.shape
    assert tq == tk, "the diagonal tile mask assumes square tiles"
    n_q = s_len // tq
    kern = functools.partial(_attn_kernel, tq=tq, tk=tk, lambda_init=lambda_init)
    vec = lambda n: pl.BlockSpec((1, n), lambda h, i: (0, 0))
    return pl.pallas_call(
        kern,
        out_shape=jax.ShapeDtypeStruct((s_len, n_heads * D_VALUE), BF16),
        grid=(n_heads, s_len // tq),
        in_specs=[vec(D_HEAD), vec(D_HEAD), vec(D_HEAD), vec(D_HEAD), vec(D_VALUE),
                  pl.BlockSpec((1, LANES, tq), lambda h, i: (h, 0, i)),
                  pl.BlockSpec((1, LANES, tq), lambda h, i: (h, 0, jnp.minimum(i + 1, n_q - 1))),
                  pl.BlockSpec((1, s_len, LANES), lambda h, i: (n_heads + h, 0, 0)),
                  pl.BlockSpec((1, s_len // tk, D_VALUE + ONES_ROWS, tk),
                               lambda h, i: (h, 0, 0, 0)),
                  pl.BlockSpec((1, tq, LANES), lambda h, i: (3 * n_heads + h, i, 0))],
        out_specs=pl.BlockSpec((tq, D_VALUE), lambda h, i: (i, h)),
        scratch_shapes=[pltpu.VMEM((2, D_VALUE + ONES_ROWS, tq), F32),
                        pltpu.VMEM((2, 2, tk, tq), F32),
                        pltpu.VMEM((2, 2, 1, tq), F32),
                        pltpu.VMEM((2, 2, LANES, tq), BF16)],
        compiler_params=pltpu.CompilerParams(
            dimension_semantics=("arbitrary", "arbitrary"), vmem_limit_bytes=VMEM_LIMIT_BYTES,
            ),
        name="attn",
    )(lq1, lk1, lq2, lk2, gain, q_t, q_t, proj, v_t, proj)


def _out_kernel(ya_ref, cb_ref, cc_ref, cx_ref, zb_ref, hc_ref, hx_ref, ga_ref, gb_ref, x_ref,
                gate_ref, cw_ref, pa_ref, pb_ref, wo_ref, lng_ref, lnb_ref, o_ref,
                *, tm, conv_slabs, gate_slabs, alpha):
    i = pl.program_id(0)
    row = lax.broadcasted_iota(jnp.int32, (tm, LANES), 0)
    have_prev = (i > 0).astype(F32)
    yb = []
    for s in range(conv_slabs):
        u = cc_ref[s].astype(F32) * cx_ref[s].astype(F32)
        prev = hc_ref[s].astype(F32) * hx_ref[s].astype(F32) * have_prev
        p1 = prev[SUBLANES - 1:SUBLANES, :]
        p2 = prev[SUBLANES - 2:SUBLANES - 1, :]
        u1 = jnp.where(row == 0, p1, pltpu.roll(u, 1, axis=0))
        u2 = jnp.where(row == 0, p2, jnp.where(row == 1, p1, pltpu.roll(u, 2, axis=0)))
        w = cw_ref[:, s * LANES:(s + 1) * LANES]
        conv = w[0:1, :] * u2 + w[1:2, :] * u1 + w[2:3, :] * u
        z = zb_ref[s].astype(F32)
        yb.append((cb_ref[s].astype(F32) * conv * (z * _sigmoid(z))).astype(BF16))
    y_b = jnp.concatenate(yb, axis=1)
    ha = jnp.dot(ya_ref[...], pa_ref[...], preferred_element_type=F32)
    hb = jnp.dot(y_b, pb_ref[...], preferred_element_type=F32)
    g_a = jnp.concatenate([ga_ref[s] for s in range(gate_slabs)], axis=1).astype(F32)
    g_b = jnp.concatenate([gb_ref[s] for s in range(gate_slabs)], axis=1).astype(F32)
    h = (_sigmoid(g_a) * ha + _sigmoid(g_b) * hb).astype(BF16)
    out = jnp.dot(h, wo_ref[...], preferred_element_type=F32)
    r = alpha * x_ref[...] + gate_ref[...] * out
    mu = jnp.mean(r, axis=1, keepdims=True)
    rc = r - mu
    var = jnp.mean(rc * rc, axis=1, keepdims=True)
    o_ref[...] = rc * lax.rsqrt(var + NORM_EPS) * lng_ref[...] + lnb_ref[...]


def _out_stage(y_a, proj, x2, gate, conv_w, pa_bf, pb_bf, wo_bf, ln_g, ln_b, alpha,
               seg_off, tm=256):
    s_len, d = x2.shape
    conv_w_width = conv_w.shape[1]
    conv_slabs = conv_w_width // LANES
    gate_slabs = d // LANES
    off_cb, off_cc, off_cx, off_zb, off_ga, off_gb = [o // LANES for o in seg_off]
    kern = functools.partial(_out_kernel, tm=tm, conv_slabs=conv_slabs, gate_slabs=gate_slabs,
                             alpha=alpha)
    slab = lambda nsl, off: pl.BlockSpec((nsl, tm, LANES), lambda i: (off // nsl, i, 0))
    halo = lambda off: pl.BlockSpec(
        (conv_slabs, SUBLANES, LANES),
        lambda i: (off // conv_slabs, jnp.maximum(i * (tm // SUBLANES) - 1, 0), 0))
    const = lambda shape: pl.BlockSpec(shape, lambda i: (0, 0), pipeline_mode=pl.Buffered(1))
    return pl.pallas_call(
        kern,
        out_shape=jax.ShapeDtypeStruct((s_len, d), F32),
        grid=(s_len // tm,),
        in_specs=[pl.BlockSpec((tm, y_a.shape[1]), lambda i: (i, 0)),
                  slab(conv_slabs, off_cb), slab(conv_slabs, off_cc), slab(conv_slabs, off_cx),
                  slab(conv_slabs, off_zb), halo(off_cc), halo(off_cx),
                  slab(gate_slabs, off_ga), slab(gate_slabs, off_gb),
                  pl.BlockSpec((tm, d), lambda i: (i, 0)),
                  const((1, d)), const((CONV_SIZE, conv_w_width)),
                  const(pa_bf.shape), const(pb_bf.shape), const(wo_bf.shape),
                  const((1, d)), const((1, d))],
        out_specs=pl.BlockSpec((tm, d), lambda i: (i, 0)),
        compiler_params=pltpu.CompilerParams(
            dimension_semantics=("arbitrary",), vmem_limit_bytes=VMEM_LIMIT_BYTES),
        name="out_stage",
    )(y_a, proj, proj, proj, proj, proj, proj, proj, proj, x2, gate, conv_w, pa_bf, pb_bf, wo_bf,
      ln_g, ln_b)


def _rope_tables(positions_1d):
    inv_freq = ROPE_THETA ** (-jnp.arange(0, D_HEAD, 2, dtype=F32) / D_HEAD)
    ang = positions_1d.astype(F32)[:, None] * inv_freq
    cos, sin = jnp.cos(ang), jnp.sin(ang)
    reps = LANES // D_HEAD
    cos_t = jnp.tile(jnp.concatenate([cos, cos], axis=1), (1, reps))
    sin_t = jnp.tile(jnp.concatenate([-sin, sin], axis=1), (1, reps))
    return cos_t, sin_t


def kernel(x, c, positions, w_mod, b_mod, w_in, lambda_q1, lambda_k1, lambda_q2, lambda_k2,
           subln_gain, conv_w, w_proj_a, w_proj_b, w_out, ln_gain, ln_bias):
    batch, s_len, d = x.shape
    depth = w_mod.shape[0]
    attn_width = w_proj_a.shape[1]
    conv_width = w_proj_b.shape[1]
    n_heads = attn_width // D_VALUE
    alpha = (2.0 * depth) ** 0.25
    seg_off = (4 * attn_width, 4 * attn_width + conv_width, 4 * attn_width + 2 * conv_width,
               4 * attn_width + 3 * conv_width, 4 * attn_width + 4 * conv_width,
               4 * attn_width + 4 * conv_width + d)
    outs = []
    for b in range(batch):
        h = x[b]
        cos_t, sin_t = _rope_tables(positions[b])
        c_col = c[b].reshape(d, 1)
        for l in range(depth):
            mod = _mod(c_col, w_mod[l], b_mod[l].reshape(1, 3 * d))
            shift, scale, gate = mod[:, :d], mod[:, d:2 * d], mod[:, 2 * d:]
            proj, q_t, v_t = _in_proj(h, scale, shift, cos_t, sin_t, w_in[l].astype(BF16),
                                      attn_width, ATTN_TK)
            y_a = _attention(proj, q_t, v_t,
                             lambda_q1[l].reshape(1, D_HEAD), lambda_k1[l].reshape(1, D_HEAD),
                             lambda_q2[l].reshape(1, D_HEAD), lambda_k2[l].reshape(1, D_HEAD),
                             subln_gain[l].reshape(1, D_VALUE), n_heads, _lambda_init(l),
                             ATTN_TQ, ATTN_TK)
            h = _out_stage(y_a, proj, h, gate, conv_w[l], w_proj_a[l].astype(BF16),
                           w_proj_b[l].astype(BF16), w_out[l].astype(BF16),
                           ln_gain[l].reshape(1, d), ln_bias[l].reshape(1, d), alpha, seg_off)
        outs.append(h)
    return outs[0][None] if batch == 1 else jnp.stack(outs, axis=0)
```

```python
import functools
import math

import jax
import jax.numpy as jnp
from jax import lax
from jax.experimental import pallas as pl
from jax.experimental.pallas import tpu as pltpu

D_HEAD = 64
D_VALUE = 2 * D_HEAD
CONV_SIZE = 3
ROPE_THETA = 10000.0
NORM_EPS = 1e-5

LANES = 128
SUBLANES = 8
MXU_N = 256
VMEM_LIMIT_BYTES = 56 * 1024 * 1024
ATTN_TQ = 512
ATTN_TK = 512
QT_PER_STEP = 2
ONES_ROWS = 16
LOG2_E = math.log2(math.e)

F32 = jnp.float32
BF16 = jnp.bfloat16


def _lambda_init(layer):
    return 0.8 - 0.6 * math.exp(-0.3 * layer)


def _sigmoid(z):
    return 1.0 / (1.0 + jnp.exp(-z))


def _mod_kernel(c_ref, w_ref, b_ref, o_ref):
    o_ref[...] = jnp.sum(w_ref[...] * c_ref[...], axis=0, keepdims=True) + b_ref[...]


def _mod(c_col, w_mod, b_mod, tn=512):
    d, n = w_mod.shape
    return pl.pallas_call(
        _mod_kernel,
        out_shape=jax.ShapeDtypeStruct((1, n), F32),
        grid=(n // tn,),
        in_specs=[pl.BlockSpec((d, 1), lambda j: (0, 0)),
                  pl.BlockSpec((d, tn), lambda j: (0, j)),
                  pl.BlockSpec((1, tn), lambda j: (0, j))],
        out_specs=pl.BlockSpec((1, tn), lambda j: (0, j)),
        compiler_params=pltpu.CompilerParams(
            dimension_semantics=("arbitrary",), vmem_limit_bytes=VMEM_LIMIT_BYTES),
        name="mod",
    )(c_col, w_mod, b_mod)


def _inproj_kernel(x_ref, scale_ref, shift_ref, cos_ref, sin_ref, w_ref, o_ref, qt_ref, vt_ref, u_sc,
                   *, slabs, tq, tk):
    j = pl.program_id(1)
    tm = x_ref.shape[0]

    @pl.when(j == 0)
    def _():
        u_sc[...] = (x_ref[...] * (1.0 + scale_ref[...]) + shift_ref[...]).astype(BF16)

    def project(epilogue):
        for c in range(slabs * LANES // MXU_N):
            acc = jnp.dot(u_sc[...], w_ref[:, c * MXU_N:(c + 1) * MXU_N],
                          preferred_element_type=F32)
            for h in range(MXU_N // LANES):
                o_ref[c * (MXU_N // LANES) + h] = epilogue(
                    acc[:, h * LANES:(h + 1) * LANES]).astype(BF16)

    @pl.when(j < 2)
    def _():
        qscale = jnp.where(j == 0, D_HEAD ** -0.5 * LOG2_E, 1.0).astype(F32)
        lane = lax.broadcasted_iota(jnp.int32, (tm, LANES), 1)
        first_half = (lane % D_HEAD) < (D_HEAD // 2)

        def rope(t):
            partner = jnp.where(first_half,
                                pltpu.roll(t, LANES - D_HEAD // 2, axis=1),
                                pltpu.roll(t, D_HEAD // 2, axis=1))
            return (t * cos_ref[...] + partner * sin_ref[...]) * qscale
        project(rope)

    @pl.when(j >= 2)
    def _():
        project(lambda t: t)

    @pl.when(j == 0)
    def _():
        for s in range(slabs):
            t = o_ref[s].astype(F32)
            for qblk in range(tm // tq):
                qt_ref[s, qblk] = t[qblk * tq:(qblk + 1) * tq, :].T.astype(BF16)

    @pl.when(j == 2)
    def _():
        for s in range(slabs):
            t = o_ref[s].astype(F32)
            for cblk in range(tm // tk):
                vt_ref[s, cblk, :LANES, :] = t[cblk * tk:(cblk + 1) * tk, :].T.astype(BF16)
                vt_ref[s, cblk, LANES:, :] = jnp.ones((ONES_ROWS, tk), BF16)


def _in_proj(x2, scale, shift, cos_t, sin_t, w_in_bf, attn_width, tq, tk, tm=1024):
    s_len, d = x2.shape
    n = w_in_bf.shape[1]
    tn = attn_width
    slabs = tn // LANES
    kern = functools.partial(_inproj_kernel, slabs=slabs, tq=tq, tk=tk)
    return pl.pallas_call(
        kern,
        out_shape=(jax.ShapeDtypeStruct((n // LANES, s_len, LANES), BF16),
                   jax.ShapeDtypeStruct((slabs, s_len // tq, LANES, tq), BF16),
                   jax.ShapeDtypeStruct((slabs, s_len // tk, LANES + ONES_ROWS, tk), BF16)),
        grid=(s_len // tm, n // tn),
        in_specs=[pl.BlockSpec((tm, d), lambda i, j: (i, 0)),
                  pl.BlockSpec((1, d), lambda i, j: (0, 0)),
                  pl.BlockSpec((1, d), lambda i, j: (0, 0)),
                  pl.BlockSpec((tm, LANES), lambda i, j: (i, 0)),
                  pl.BlockSpec((tm, LANES), lambda i, j: (i, 0)),
                  pl.BlockSpec((d, tn), lambda i, j: (0, j))],
        out_specs=(pl.BlockSpec((slabs, tm, LANES), lambda i, j: (j, i, 0)),
                   pl.BlockSpec((slabs, tm // tq, LANES, tq), lambda i, j: (0, i, 0, 0)),
                   pl.BlockSpec((slabs, tm // tk, LANES + ONES_ROWS, tk), lambda i, j: (0, i, 0, 0))),
        scratch_shapes=[pltpu.VMEM((tm, d), BF16)],
        compiler_params=pltpu.CompilerParams(
            dimension_semantics=("arbitrary", "arbitrary"), vmem_limit_bytes=VMEM_LIMIT_BYTES),
        name="in_proj",
    )(x2, scale, shift, cos_t, sin_t, w_in_bf)


def _attn_kernel(lq1_ref, lk1_ref, lq2_ref, lk2_ref, gain_ref, qt_ref, qtn_ref, k_ref, vt_ref, z_ref,
                 o_ref, acc_sc, s_sc, mx_sc, qm_sc, *, tq, tk, lambda_init):
    g = pl.program_id(1)

    def one_tile(t, carry):
        i = QT_PER_STEP * g + t
        drow = lax.broadcasted_iota(jnp.int32, (LANES, tq), 0)
        zero = jnp.zeros((LANES, tq), BF16)
        q_nxt = jnp.where(t + 1 < QT_PER_STEP,
                          qt_ref[0, jnp.minimum(t + 1, QT_PER_STEP - 1)], qtn_ref[0, 0])
        for which, qv in enumerate((qt_ref[0, t], q_nxt)):
            qm_sc[which, 0] = jnp.where(drow < D_HEAD, qv, zero)
            qm_sc[which, 1] = jnp.where(drow >= D_HEAD, qv, zero)
        acc_sc[...] = jnp.zeros(acc_sc.shape, F32)
        units = [(mp, c) for mp in range(2) for c in range(tq // MXU_N)]
        cols = lambda c: slice(c * MXU_N, (c + 1) * MXU_N)

        def qk_unit(j, slot, mp, c, which=0):
            k = k_ref[0, pl.ds(pl.multiple_of(j * tk, tk), tk), :]
            s = jnp.dot(k, qm_sc[which, mp, :, cols(c)], preferred_element_type=F32)
            s_sc[slot, mp, :, cols(c)] = s
            mx_sc[slot, mp, :, cols(c)] = jnp.max(s, axis=0, keepdims=True)

        def softmax_pv_unit(j, slot, mp, c, m_old, masked):
            s = s_sc[slot, mp, :, cols(c)]
            if masked:
                key = lax.broadcasted_iota(jnp.int32, s.shape, 0)
                qry = lax.broadcasted_iota(jnp.int32, s.shape, 1) + c * MXU_N
                s = jnp.where(key <= qry, s, -jnp.inf)
                m_cur = jnp.max(s, axis=0, keepdims=True)
            else:
                m_cur = mx_sc[slot, mp, :, cols(c)]
            m_new = jnp.maximum(m_old, m_cur)
            p = jnp.exp2((s - m_new).astype(BF16))
            alpha = jnp.exp2(m_old - m_new)
            acc_sc[mp, :, cols(c)] = alpha * acc_sc[mp, :, cols(c)] + jnp.dot(
                vt_ref[0, j], p, preferred_element_type=F32)
            return m_new

        def stage(j_next, slot_next, j, slot, m, masked):
            m_out = []
            for u, (mp, c) in enumerate(units):
                qk_unit(j_next, slot_next, mp, c)
                m_out.append(softmax_pv_unit(j, slot, mp, c, m[u], masked))
            return tuple(m_out)

        def last_stage(slot, m):
            for u, (mp, c) in enumerate(units):
                softmax_pv_unit(i, slot, mp, c, m[u], True)
                qk_unit(0, 0, mp, c, which=1)

        def pair(jj, m):
            j = 2 * jj
            m = stage(j + 1, 1, j, 0, m, False)
            return stage(j + 2, 0, j + 1, 1, m, False)

        @pl.when(i == 0)
        def _():
            for mp, c in units:
                qk_unit(0, 0, mp, c)

        neg = jnp.full((1, MXU_N), -jnp.inf, F32)
        quad = lambda q, m: pair(2 * q + 1, pair(2 * q, m))
        m = lax.fori_loop(0, i // 8, lambda o, m: quad(2 * o + 1, quad(2 * o, m)), (neg,) * len(units))
        m = lax.fori_loop(2 * (i // 8), i // 4, quad, m)
        m = lax.fori_loop(2 * (i // 4), i // 2, pair, m)

        def finalize():
            lam = (jnp.exp(jnp.sum(lq1_ref[...] * lk1_ref[...], axis=1, keepdims=True))
                   - jnp.exp(jnp.sum(lq2_ref[...] * lk2_ref[...], axis=1, keepdims=True))
                   + lambda_init)
            a0, a1 = acc_sc[0], acc_sc[1]
            ot = (a0[:D_VALUE] * (1.0 / a0[D_VALUE:D_VALUE + 1])
                  - lam * (a1[:D_VALUE] * (1.0 / a1[D_VALUE:D_VALUE + 1])))
            o = ot.T
            o = o * lax.rsqrt(jnp.mean(o * o, axis=1, keepdims=True) + NORM_EPS)
            o = o * gain_ref[...] * (1.0 - lambda_init)
            rows = pl.ds(pl.multiple_of(t * tq, tq), tq)
            z = z_ref[0, rows, :].astype(F32)
            o_ref[rows, :] = (o * (z * _sigmoid(z))).astype(o_ref.dtype)

        @pl.when(i % 2 == 1)
        def _():
            last_stage(1, stage(i, 1, i - 1, 0, m, False))
            finalize()

        @pl.when(i % 2 == 0)
        def _():
            last_stage(0, m)
            finalize()
        return carry

    lax.fori_loop(0, QT_PER_STEP, one_tile, 0)


def _attention(proj, q_t, v_t, lq1, lk1, lq2, lk2, gain, n_heads, lambda_init, tq, tk):
    n_slabs, s_len, _ = proj.shape
    assert tq == tk, "the diagonal tile mask assumes square tiles"
    n_q = s_len // tq
    kern = functools.partial(_attn_kernel, tq=tq, tk=tk, lambda_init=lambda_init)
    vec = lambda n: pl.BlockSpec((1, n), lambda h, g: (0, 0))
    return pl.pallas_call(
        kern,
        out_shape=jax.ShapeDtypeStruct((s_len, n_heads * D_VALUE), BF16),
        grid=(n_heads, n_q // QT_PER_STEP),
        in_specs=[vec(D_HEAD), vec(D_HEAD), vec(D_HEAD), vec(D_HEAD), vec(D_VALUE),
                  pl.BlockSpec((1, QT_PER_STEP, LANES, tq), lambda h, g: (h, g, 0, 0)),
                  pl.BlockSpec((1, 1, LANES, tq),
                               lambda h, g: (h, jnp.minimum(QT_PER_STEP * (g + 1), n_q - 1), 0, 0)),
                  pl.BlockSpec((1, s_len, LANES), lambda h, g: (n_heads + h, 0, 0)),
                  pl.BlockSpec((1, s_len // tk, D_VALUE + ONES_ROWS, tk),
                               lambda h, g: (h, 0, 0, 0)),
                  pl.BlockSpec((1, QT_PER_STEP * tq, LANES),
                               lambda h, g: (3 * n_heads + h, g, 0))],
        out_specs=pl.BlockSpec((QT_PER_STEP * tq, D_VALUE), lambda h, g: (g, h)),
        scratch_shapes=[pltpu.VMEM((2, D_VALUE + ONES_ROWS, tq), F32),
                        pltpu.VMEM((2, 2, tk, tq), F32),
                        pltpu.VMEM((2, 2, 1, tq), F32),
                        pltpu.VMEM((2, 2, LANES, tq), BF16)],
        compiler_params=pltpu.CompilerParams(
            dimension_semantics=("arbitrary", "arbitrary"), vmem_limit_bytes=VMEM_LIMIT_BYTES),
        name="attn",
    )(lq1, lk1, lq2, lk2, gain, q_t, q_t, proj, v_t, proj)


def _out_kernel(ya_ref, cb_ref, cc_ref, cx_ref, zb_ref, hc_ref, hx_ref, ga_ref, gb_ref, x_ref,
                gate_ref, cw_ref, pa_ref, pb_ref, wo_ref, lng_ref, lnb_ref, o_ref,
                *, tm, conv_slabs, gate_slabs, alpha):
    i = pl.program_id(0)
    row = lax.broadcasted_iota(jnp.int32, (tm, LANES), 0)
    have_prev = (i > 0).astype(F32)
    yb = []
    for s in range(conv_slabs):
        u = cc_ref[s].astype(F32) * cx_ref[s].astype(F32)
        prev = hc_ref[s].astype(F32) * hx_ref[s].astype(F32) * have_prev
        p1 = prev[SUBLANES - 1:SUBLANES, :]
        p2 = prev[SUBLANES - 2:SUBLANES - 1, :]
        u1 = jnp.where(row == 0, p1, pltpu.roll(u, 1, axis=0))
        u2 = jnp.where(row == 0, p2, jnp.where(row == 1, p1, pltpu.roll(u, 2, axis=0)))
        w = cw_ref[:, s * LANES:(s + 1) * LANES]
        conv = w[0:1, :] * u2 + w[1:2, :] * u1 + w[2:3, :] * u
        z = zb_ref[s].astype(F32)
        yb.append((cb_ref[s].astype(F32) * conv * (z * _sigmoid(z))).astype(BF16))
    y_b = jnp.concatenate(yb, axis=1)
    ha = jnp.dot(ya_ref[...], pa_ref[...], preferred_element_type=F32)
    hb = jnp.dot(y_b, pb_ref[...], preferred_element_type=F32)
    g_a = jnp.concatenate([ga_ref[s] for s in range(gate_slabs)], axis=1).astype(F32)
    g_b = jnp.concatenate([gb_ref[s] for s in range(gate_slabs)], axis=1).astype(F32)
    h = (_sigmoid(g_a) * ha + _sigmoid(g_b) * hb).astype(BF16)
    out = jnp.dot(h, wo_ref[...], preferred_element_type=F32)
    r = alpha * x_ref[...] + gate_ref[...] * out
    mu = jnp.mean(r, axis=1, keepdims=True)
    rc = r - mu
    var = jnp.mean(rc * rc, axis=1, keepdims=True)
    o_ref[...] = rc * lax.rsqrt(var + NORM_EPS) * lng_ref[...] + lnb_ref[...]


def _out_stage(y_a, proj, x2, gate, conv_w, pa_bf, pb_bf, wo_bf, ln_g, ln_b, alpha,
               seg_off, tm=256):
    s_len, d = x2.shape
    conv_w_width = conv_w.shape[1]
    conv_slabs = conv_w_width // LANES
    gate_slabs = d // LANES
    off_cb, off_cc, off_cx, off_zb, off_ga, off_gb = [o // LANES for o in seg_off]
    kern = functools.partial(_out_kernel, tm=tm, conv_slabs=conv_slabs, gate_slabs=gate_slabs,
                             alpha=alpha)
    slab = lambda nsl, off: pl.BlockSpec((nsl, tm, LANES), lambda i: (off // nsl, i, 0))
    halo = lambda off: pl.BlockSpec(
        (conv_slabs, SUBLANES, LANES),
        lambda i: (off // conv_slabs, jnp.maximum(i * (tm // SUBLANES) - 1, 0), 0))
    const = lambda shape: pl.BlockSpec(shape, lambda i: (0, 0), pipeline_mode=pl.Buffered(1))
    return pl.pallas_call(
        kern,
        out_shape=jax.ShapeDtypeStruct((s_len, d), F32),
        grid=(s_len // tm,),
        in_specs=[pl.BlockSpec((tm, y_a.shape[1]), lambda i: (i, 0)),
                  slab(conv_slabs, off_cb), slab(conv_slabs, off_cc), slab(conv_slabs, off_cx),
                  slab(conv_slabs, off_zb), halo(off_cc), halo(off_cx),
                  slab(gate_slabs, off_ga), slab(gate_slabs, off_gb),
                  pl.BlockSpec((tm, d), lambda i: (i, 0)),
                  const((1, d)), const((CONV_SIZE, conv_w_width)),
                  const(pa_bf.shape), const(pb_bf.shape), const(wo_bf.shape),
                  const((1, d)), const((1, d))],
        out_specs=pl.BlockSpec((tm, d), lambda i: (i, 0)),
        compiler_params=pltpu.CompilerParams(
            dimension_semantics=("arbitrary",), vmem_limit_bytes=VMEM_LIMIT_BYTES),
        name="out_stage",
    )(y_a, proj, proj, proj, proj, proj, proj, proj, proj, x2, gate, conv_w, pa_bf, pb_bf, wo_bf,
      ln_g, ln_b)


def _rope_tables(positions_1d):
    lane = jnp.arange(LANES)
    inv_freq = ROPE_THETA ** (-(2 * (lane % (D_HEAD // 2))).astype(F32) / D_HEAD)
    sign = jnp.where((lane % D_HEAD) < D_HEAD // 2, -1.0, 1.0).astype(F32)
    ang = positions_1d.astype(F32)[:, None] * inv_freq
    return jnp.cos(ang), jnp.sin(ang) * sign


def kernel(x, c, positions, w_mod, b_mod, w_in, lambda_q1, lambda_k1, lambda_q2, lambda_k2,
           subln_gain, conv_w, w_proj_a, w_proj_b, w_out, ln_gain, ln_bias):
    batch, s_len, d = x.shape
    depth = w_mod.shape[0]
    attn_width = w_proj_a.shape[1]
    conv_width = w_proj_b.shape[1]
    n_heads = attn_width // D_VALUE
    alpha = (2.0 * depth) ** 0.25
    seg_off = (4 * attn_width, 4 * attn_width + conv_width, 4 * attn_width + 2 * conv_width,
               4 * attn_width + 3 * conv_width, 4 * attn_width + 4 * conv_width,
               4 * attn_width + 4 * conv_width + d)
    outs = []
    for b in range(batch):
        h = x[b]
        cos_t, sin_t = _rope_tables(positions[b])
        c_col = c[b].reshape(d, 1)
        for l in range(depth):
            mod = _mod(c_col, w_mod[l], b_mod[l].reshape(1, 3 * d))
            shift, scale, gate = mod[:, :d], mod[:, d:2 * d], mod[:, 2 * d:]
            proj, q_t, v_t = _in_proj(h, scale, shift, cos_t, sin_t, w_in[l].astype(BF16),
                                      attn_width, ATTN_TQ, ATTN_TK)
            y_a = _attention(proj, q_t, v_t,
                             lambda_q1[l].reshape(1, D_HEAD), lambda_k1[l].reshape(1, D_HEAD),
                             lambda_q2[l].reshape(1, D_HEAD), lambda_k2[l].reshape(1, D_HEAD),
                             subln_gain[l].reshape(1, D_VALUE), n_heads, _lambda_init(l),
                             ATTN_TQ, ATTN_TK)
            h = _out_stage(y_a, proj, h, gate, conv_w[l], w_proj_a[l].astype(BF16),
                           w_proj_b[l].astype(BF16), w_out[l].astype(BF16),
                           ln_gain[l].reshape(1, d), ln_bias[l].reshape(1, d), alpha, seg_off)
        outs.append(h)
    return outs[0][None] if batch == 1 else jnp.stack(outs, axis=0)
```

```python
import functools
import math

import jax
import jax.numpy as jnp
from jax import lax
from jax.experimental import pallas as pl
from jax.experimental.pallas import tpu as pltpu

D_HEAD = 64
D_VALUE = 2 * D_HEAD
CONV_SIZE = 3
ROPE_THETA = 10000.0
NORM_EPS = 1e-5

LANES = 128
SUBLANES = 8
MXU_N = 256
V7X_VMEM_BYTES = 64 * 1024 * 1024
VMEM_LIMIT_BYTES = V7X_VMEM_BYTES * 7 // 8
ATTN_TQ = 512
ATTN_TK = 512
QT_PER_STEP = 2
ONES_ROWS = 16
LOG2_E = math.log2(math.e)

F32 = jnp.float32
BF16 = jnp.bfloat16


def _lambda_init(layer):
    return 0.8 - 0.6 * math.exp(-0.3 * layer)


def _sigmoid(z):
    return 1.0 / (1.0 + jnp.exp(-z))


def _mod_kernel(c_ref, w_ref, b_ref, o_ref):
    o_ref[...] = jnp.sum(w_ref[...] * c_ref[...], axis=0, keepdims=True) + b_ref[...]


def _mod(c_col, w_mod, b_mod, tn=512):
    d, n = w_mod.shape
    return pl.pallas_call(
        _mod_kernel,
        out_shape=jax.ShapeDtypeStruct((1, n), F32),
        grid=(n // tn,),
        in_specs=[pl.BlockSpec((d, 1), lambda j: (0, 0)),
                  pl.BlockSpec((d, tn), lambda j: (0, j)),
                  pl.BlockSpec((1, tn), lambda j: (0, j))],
        out_specs=pl.BlockSpec((1, tn), lambda j: (0, j)),
        compiler_params=pltpu.CompilerParams(
            dimension_semantics=("arbitrary",), vmem_limit_bytes=VMEM_LIMIT_BYTES),
        name="mod",
    )(c_col, w_mod, b_mod)


def _inproj_kernel(x_ref, scale_ref, shift_ref, cos_ref, sin_ref, w_ref, o_ref, qt_ref, vt_ref, u_sc,
                   *, slabs, tq, tk):
    j = pl.program_id(1)
    tm = x_ref.shape[0]

    @pl.when(j == 0)
    def _():
        u_sc[...] = (x_ref[...] * (1.0 + scale_ref[...]) + shift_ref[...]).astype(BF16)

    def project(epilogue, transposed=None):
        for c in range(slabs * LANES // MXU_N):
            acc = jnp.dot(u_sc[...], w_ref[:, c * MXU_N:(c + 1) * MXU_N],
                          preferred_element_type=F32)
            for h in range(MXU_N // LANES):
                s = c * (MXU_N // LANES) + h
                r = epilogue(acc[:, h * LANES:(h + 1) * LANES])
                o_ref[s] = r.astype(BF16)
                if transposed is not None:
                    transposed(s, r)

    def rope(qscale):
        lane = lax.broadcasted_iota(jnp.int32, (tm, LANES), 1)
        first_half = (lane % D_HEAD) < (D_HEAD // 2)

        def apply(t):
            partner = jnp.where(first_half,
                                pltpu.roll(t, LANES - D_HEAD // 2, axis=1),
                                pltpu.roll(t, D_HEAD // 2, axis=1))
            r = t * cos_ref[...] + partner * sin_ref[...]
            return r if qscale is None else r * qscale
        return apply

    def store_qt(s, r):
        for qblk in range(tm // tq):
            qt_ref[s, qblk] = r[qblk * tq:(qblk + 1) * tq, :].T.astype(BF16)

    def store_vt(s, r):
        for cblk in range(tm // tk):
            vt_ref[s, cblk, :LANES, :] = r[cblk * tk:(cblk + 1) * tk, :].T.astype(BF16)
            vt_ref[s, cblk, LANES:, :] = jnp.ones((ONES_ROWS, tk), BF16)

    @pl.when(j == 0)
    def _():
        project(rope(D_HEAD ** -0.5 * LOG2_E), store_qt)

    @pl.when(j == 1)
    def _():
        project(rope(None))

    @pl.when(j == 2)
    def _():
        project(lambda t: t, store_vt)

    @pl.when(j >= 3)
    def _():
        project(lambda t: t)


def _in_proj(x2, scale, shift, cos_t, sin_t, w_in_bf, attn_width, tq, tk, tm=1024):
    s_len, d = x2.shape
    n = w_in_bf.shape[1]
    tn = attn_width
    slabs = tn // LANES
    kern = functools.partial(_inproj_kernel, slabs=slabs, tq=tq, tk=tk)
    return pl.pallas_call(
        kern,
        out_shape=(jax.ShapeDtypeStruct((n // LANES, s_len, LANES), BF16),
                   jax.ShapeDtypeStruct((slabs, s_len // tq, LANES, tq), BF16),
                   jax.ShapeDtypeStruct((slabs, s_len // tk, LANES + ONES_ROWS, tk), BF16)),
        grid=(s_len // tm, n // tn),
        in_specs=[pl.BlockSpec((tm, d), lambda i, j: (i, 0)),
                  pl.BlockSpec((1, d), lambda i, j: (0, 0)),
                  pl.BlockSpec((1, d), lambda i, j: (0, 0)),
                  pl.BlockSpec((tm, LANES), lambda i, j: (i, 0)),
                  pl.BlockSpec((tm, LANES), lambda i, j: (i, 0)),
                  pl.BlockSpec((d, tn), lambda i, j: (0, j))],
        out_specs=(pl.BlockSpec((slabs, tm, LANES), lambda i, j: (j, i, 0)),
                   pl.BlockSpec((slabs, tm // tq, LANES, tq), lambda i, j: (0, i, 0, 0)),
                   pl.BlockSpec((slabs, tm // tk, LANES + ONES_ROWS, tk), lambda i, j: (0, i, 0, 0))),
        scratch_shapes=[pltpu.VMEM((tm, d), BF16)],
        compiler_params=pltpu.CompilerParams(
            dimension_semantics=("arbitrary", "arbitrary"), vmem_limit_bytes=VMEM_LIMIT_BYTES),
        name="in_proj",
    )(x2, scale, shift, cos_t, sin_t, w_in_bf)


def _attn_kernel(lq1_ref, lk1_ref, lq2_ref, lk2_ref, gain_ref, qt_ref, qtn_ref, k_ref, vt_ref, z_ref,
                 o_ref, acc_sc, s_sc, mx_sc, qm_sc, *, tq, tk, lambda_init):
    g = pl.program_id(1)

    def one_tile(t, carry):
        i = QT_PER_STEP * g + t
        drow = lax.broadcasted_iota(jnp.int32, (LANES, tq), 0)
        zero = jnp.zeros((LANES, tq), BF16)
        q_nxt = jnp.where(t + 1 < QT_PER_STEP,
                          qt_ref[0, jnp.minimum(t + 1, QT_PER_STEP - 1)], qtn_ref[0, 0])
        for which, qv in enumerate((qt_ref[0, t], q_nxt)):
            qm_sc[which, 0] = jnp.where(drow < D_HEAD, qv, zero)
            qm_sc[which, 1] = jnp.where(drow >= D_HEAD, qv, zero)
        acc_sc[...] = jnp.zeros(acc_sc.shape, F32)
        units = [(mp, c) for mp in range(2) for c in range(tq // MXU_N)]
        cols = lambda c: slice(c * MXU_N, (c + 1) * MXU_N)

        def qk_unit(j, slot, mp, c, which=0):
            k = k_ref[0, pl.ds(pl.multiple_of(j * tk, tk), tk), :]
            s = jnp.dot(k, qm_sc[which, mp, :, cols(c)], preferred_element_type=F32)
            s_sc[slot, mp, :, cols(c)] = s
            mx_sc[slot, mp, :, cols(c)] = jnp.max(s, axis=0, keepdims=True)

        def softmax_pv_unit(j, slot, mp, c, m_old, masked):
            s = s_sc[slot, mp, :, cols(c)]
            if masked:
                key = lax.broadcasted_iota(jnp.int32, s.shape, 0)
                qry = lax.broadcasted_iota(jnp.int32, s.shape, 1) + c * MXU_N
                s = jnp.where(key <= qry, s, -jnp.inf)
                m_cur = jnp.max(s, axis=0, keepdims=True)
            else:
                m_cur = mx_sc[slot, mp, :, cols(c)]
            m_new = jnp.maximum(m_old, m_cur)
            p = jnp.exp2((s - m_new).astype(BF16))
            alpha = jnp.exp2(m_old - m_new)
            acc_sc[mp, :, cols(c)] = alpha * acc_sc[mp, :, cols(c)] + jnp.dot(
                vt_ref[0, j], p, preferred_element_type=F32)
            return m_new

        def stage(j_next, slot_next, j, slot, m, masked):
            m_out = []
            for u, (mp, c) in enumerate(units):
                qk_unit(j_next, slot_next, mp, c)
                m_out.append(softmax_pv_unit(j, slot, mp, c, m[u], masked))
            return tuple(m_out)

        def last_stage(slot, m):
            for u, (mp, c) in enumerate(units):
                softmax_pv_unit(i, slot, mp, c, m[u], True)
                qk_unit(0, 0, mp, c, which=1)

        def pair(jj, m):
            j = 2 * jj
            m = stage(j + 1, 1, j, 0, m, False)
            return stage(j + 2, 0, j + 1, 1, m, False)

        @pl.when(i == 0)
        def _():
            for mp, c in units:
                qk_unit(0, 0, mp, c)

        neg = jnp.full((1, MXU_N), -jnp.inf, F32)
        quad = lambda q, m: pair(2 * q + 1, pair(2 * q, m))
        octo = lambda o, m: quad(2 * o + 1, quad(2 * o, m))
        m = lax.fori_loop(0, i // 16, lambda x, m: octo(2 * x + 1, octo(2 * x, m)), (neg,) * len(units))
        m = lax.fori_loop(2 * (i // 16), i // 8, octo, m)
        m = lax.fori_loop(2 * (i // 8), i // 4, quad, m)
        m = lax.fori_loop(2 * (i // 4), i // 2, pair, m)

        def finalize():
            lam = (jnp.exp(jnp.sum(lq1_ref[...] * lk1_ref[...], axis=1, keepdims=True))
                   - jnp.exp(jnp.sum(lq2_ref[...] * lk2_ref[...], axis=1, keepdims=True))
                   + lambda_init)
            a0, a1 = acc_sc[0], acc_sc[1]
            ot = (a0[:D_VALUE] * (1.0 / a0[D_VALUE:D_VALUE + 1])
                  - lam * (a1[:D_VALUE] * (1.0 / a1[D_VALUE:D_VALUE + 1])))
            o = ot.T
            o = o * lax.rsqrt(jnp.mean(o * o, axis=1, keepdims=True) + NORM_EPS)
            o = o * gain_ref[...] * (1.0 - lambda_init)
            rows = pl.ds(pl.multiple_of(t * tq, tq), tq)
            z = z_ref[0, rows, :].astype(F32)
            o_ref[rows, :] = (o * (z * _sigmoid(z))).astype(o_ref.dtype)

        @pl.when(i % 2 == 1)
        def _():
            last_stage(1, stage(i, 1, i - 1, 0, m, False))
            finalize()

        @pl.when(i % 2 == 0)
        def _():
            last_stage(0, m)
            finalize()
        return carry

    lax.fori_loop(0, QT_PER_STEP, one_tile, 0)


def _attention(proj, q_t, v_t, lq1, lk1, lq2, lk2, gain, n_heads, lambda_init, tq, tk):
    n_slabs, s_len, _ = proj.shape
    assert tq == tk, "the diagonal tile mask assumes square tiles"
    n_q = s_len // tq
    kern = functools.partial(_attn_kernel, tq=tq, tk=tk, lambda_init=lambda_init)
    vec = lambda n: pl.BlockSpec((1, n), lambda h, g: (0, 0))
    return pl.pallas_call(
        kern,
        out_shape=jax.ShapeDtypeStruct((s_len, n_heads * D_VALUE), BF16),
        grid=(n_heads, n_q // QT_PER_STEP),
        in_specs=[vec(D_HEAD), vec(D_HEAD), vec(D_HEAD), vec(D_HEAD), vec(D_VALUE),
                  pl.BlockSpec((1, QT_PER_STEP, LANES, tq), lambda h, g: (h, g, 0, 0)),
                  pl.BlockSpec((1, 1, LANES, tq),
                               lambda h, g: (h, jnp.minimum(QT_PER_STEP * (g + 1), n_q - 1), 0, 0)),
                  pl.BlockSpec((1, s_len, LANES), lambda h, g: (n_heads + h, 0, 0)),
                  pl.BlockSpec((1, s_len // tk, D_VALUE + ONES_ROWS, tk),
                               lambda h, g: (h, 0, 0, 0)),
                  pl.BlockSpec((1, QT_PER_STEP * tq, LANES),
                               lambda h, g: (3 * n_heads + h, g, 0))],
        out_specs=pl.BlockSpec((QT_PER_STEP * tq, D_VALUE), lambda h, g: (g, h)),
        scratch_shapes=[pltpu.VMEM((2, D_VALUE + ONES_ROWS, tq), F32),
                        pltpu.VMEM((2, 2, tk, tq), F32),
                        pltpu.VMEM((2, 2, 1, tq), F32),
                        pltpu.VMEM((2, 2, LANES, tq), BF16)],
        compiler_params=pltpu.CompilerParams(
            dimension_semantics=("arbitrary", "arbitrary"), vmem_limit_bytes=VMEM_LIMIT_BYTES),
        name="attn",
    )(lq1, lk1, lq2, lk2, gain, q_t, q_t, proj, v_t, proj)


def _out_kernel(ya_ref, cb_ref, cc_ref, cx_ref, zb_ref, hc_ref, hx_ref, ga_ref, gb_ref, x_ref,
                gate_ref, cw_ref, pa_ref, pb_ref, wo_ref, lng_ref, lnb_ref, o_ref,
                *, tm, conv_slabs, gate_slabs, alpha):
    i = pl.program_id(0)
    row = lax.broadcasted_iota(jnp.int32, (tm, LANES), 0)
    have_prev = i > 0
    yb = []
    for s in range(conv_slabs):
        u = cc_ref[s].astype(F32) * cx_ref[s].astype(F32)
        prev = jnp.where(have_prev, hc_ref[s].astype(F32) * hx_ref[s].astype(F32), 0.0)
        p1 = prev[SUBLANES - 1:SUBLANES, :]
        p2 = prev[SUBLANES - 2:SUBLANES - 1, :]
        u1 = jnp.where(row == 0, p1, pltpu.roll(u, 1, axis=0))
        u2 = jnp.where(row == 0, p2, jnp.where(row == 1, p1, pltpu.roll(u, 2, axis=0)))
        w = cw_ref[:, s * LANES:(s + 1) * LANES]
        conv = w[0:1, :] * u2 + w[1:2, :] * u1 + w[2:3, :] * u
        z = zb_ref[s].astype(F32)
        yb.append((cb_ref[s].astype(F32) * conv * (z * _sigmoid(z))).astype(BF16))
    y_b = jnp.concatenate(yb, axis=1)
    ha = jnp.dot(ya_ref[...], pa_ref[...], preferred_element_type=F32)
    hb = jnp.dot(y_b, pb_ref[...], preferred_element_type=F32)
    g_a = jnp.concatenate([ga_ref[s] for s in range(gate_slabs)], axis=1).astype(F32)
    g_b = jnp.concatenate([gb_ref[s] for s in range(gate_slabs)], axis=1).astype(F32)
    h = (_sigmoid(g_a) * ha + _sigmoid(g_b) * hb).astype(BF16)
    out = jnp.dot(h, wo_ref[...], preferred_element_type=F32)
    r = alpha * x_ref[...] + gate_ref[...] * out
    mu = jnp.mean(r, axis=1, keepdims=True)
    rc = r - mu
    var = jnp.mean(rc * rc, axis=1, keepdims=True)
    o_ref[...] = rc * lax.rsqrt(var + NORM_EPS) * lng_ref[...] + lnb_ref[...]


def _out_stage(y_a, proj, x2, gate, conv_w, pa_bf, pb_bf, wo_bf, ln_g, ln_b, alpha,
               seg_off, tm=256):
    s_len, d = x2.shape
    conv_w_width = conv_w.shape[1]
    conv_slabs = conv_w_width // LANES
    gate_slabs = d // LANES
    off_cb, off_cc, off_cx, off_zb, off_ga, off_gb = [o // LANES for o in seg_off]
    kern = functools.partial(_out_kernel, tm=tm, conv_slabs=conv_slabs, gate_slabs=gate_slabs,
                             alpha=alpha)
    slab = lambda nsl, off: pl.BlockSpec((nsl, tm, LANES), lambda i: (off // nsl, i, 0))
    halo = lambda off: pl.BlockSpec(
        (conv_slabs, SUBLANES, LANES),
        lambda i: (off // conv_slabs, jnp.maximum(i * (tm // SUBLANES) - 1, 0), 0))
    const = lambda shape: pl.BlockSpec(shape, lambda i: (0, 0), pipeline_mode=pl.Buffered(1))
    return pl.pallas_call(
        kern,
        out_shape=jax.ShapeDtypeStruct((s_len, d), F32),
        grid=(s_len // tm,),
        in_specs=[pl.BlockSpec((tm, y_a.shape[1]), lambda i: (i, 0)),
                  slab(conv_slabs, off_cb), slab(conv_slabs, off_cc), slab(conv_slabs, off_cx),
                  slab(conv_slabs, off_zb), halo(off_cc), halo(off_cx),
                  slab(gate_slabs, off_ga), slab(gate_slabs, off_gb),
                  pl.BlockSpec((tm, d), lambda i: (i, 0)),
                  const((1, d)), const((CONV_SIZE, conv_w_width)),
                  const(pa_bf.shape), const(pb_bf.shape), const(wo_bf.shape),
                  const((1, d)), const((1, d))],
        out_specs=pl.BlockSpec((tm, d), lambda i: (i, 0)),
        compiler_params=pltpu.CompilerParams(
            dimension_semantics=("arbitrary",), vmem_limit_bytes=VMEM_LIMIT_BYTES),
        name="out_stage",
    )(y_a, proj, proj, proj, proj, proj, proj, proj, proj, x2, gate, conv_w, pa_bf, pb_bf, wo_bf,
      ln_g, ln_b)


def _rope_tables(positions_1d):
    lane = jnp.arange(LANES)
    inv_freq = ROPE_THETA ** (-(2 * (lane % (D_HEAD // 2))).astype(F32) / D_HEAD)
    sign = jnp.where((lane % D_HEAD) < D_HEAD // 2, -1.0, 1.0).astype(F32)
    ang = positions_1d.astype(F32)[:, None] * inv_freq
    return jnp.cos(ang), jnp.sin(ang) * sign


def kernel(x, c, positions, w_mod, b_mod, w_in, lambda_q1, lambda_k1, lambda_q2, lambda_k2,
           subln_gain, conv_w, w_proj_a, w_proj_b, w_out, ln_gain, ln_bias):
    batch, s_len, d = x.shape
    depth = w_mod.shape[0]
    attn_width = w_proj_a.shape[1]
    conv_width = w_proj_b.shape[1]
    n_heads = attn_width // D_VALUE
    alpha = (2.0 * depth) ** 0.25
    seg_off = (4 * attn_width, 4 * attn_width + conv_width, 4 * attn_width + 2 * conv_width,
               4 * attn_width + 3 * conv_width, 4 * attn_width + 4 * conv_width,
               4 * attn_width + 4 * conv_width + d)
    outs = []
    for b in range(batch):
        h = x[b]
        cos_t, sin_t = _rope_tables(positions[b])
        c_col = c[b].reshape(d, 1)
        for l in range(depth):
            mod = _mod(c_col, w_mod[l], b_mod[l].reshape(1, 3 * d))
            shift, scale, gate = mod[:, :d], mod[:, d:2 * d], mod[:, 2 * d:]
            proj, q_t, v_t = _in_proj(h, scale, shift, cos_t, sin_t, w_in[l].astype(BF16),
                                      attn_width, ATTN_TQ, ATTN_TK)
            y_a = _attention(proj, q_t, v_t,
                             lambda_q1[l].reshape(1, D_HEAD), lambda_k1[l].reshape(1, D_HEAD),
                             lambda_q2[l].reshape(1, D_HEAD), lambda_k2[l].reshape(1, D_HEAD),
                             subln_gain[l].reshape(1, D_VALUE), n_heads, _lambda_init(l),
                             ATTN_TQ, ATTN_TK)
            h = _out_stage(y_a, proj, h, gate, conv_w[l], w_proj_a[l].astype(BF16),
                           w_proj_b[l].astype(BF16), w_out[l].astype(BF16),
                           ln_gain[l].reshape(1, d), ln_bias[l].reshape(1, d), alpha, seg_off)
        outs.append(h)
    return outs[0][None] if batch == 1 else jnp.stack(outs, axis=0)
```

```python
import functools
import math

import jax
import jax.numpy as jnp
from jax import lax
from jax.experimental import pallas as pl
from jax.experimental.pallas import tpu as pltpu

D_HEAD = 64
D_VALUE = 2 * D_HEAD
CONV_SIZE = 3
ROPE_THETA = 10000.0
NORM_EPS = 1e-5

LANES = 128
SUBLANES = 8
MXU_N = 256
V7X_VMEM_BYTES = 64 * 1024 * 1024
VMEM_LIMIT_BYTES = V7X_VMEM_BYTES * 7 // 8
ATTN_TQ = 512
ATTN_TK = 512
QT_PER_STEP = 4
ONES_ROWS = 16
LOG2_E = math.log2(math.e)

F32 = jnp.float32
BF16 = jnp.bfloat16


def _lambda_init(layer):
    return 0.8 - 0.6 * math.exp(-0.3 * layer)


def _sigmoid(z):
    return 1.0 / (1.0 + jnp.exp(-z))


def _mod_kernel(c_ref, w_ref, b_ref, o_ref):
    o_ref[...] = jnp.sum(w_ref[...] * c_ref[...], axis=0, keepdims=True) + b_ref[...]


def _mod(c_col, w_mod, b_mod, tn=1024):
    d, n = w_mod.shape
    return pl.pallas_call(
        _mod_kernel,
        out_shape=jax.ShapeDtypeStruct((1, n), F32),
        grid=(n // tn,),
        in_specs=[pl.BlockSpec((d, 1), lambda j: (0, 0)),
                  pl.BlockSpec((d, tn), lambda j: (0, j)),
                  pl.BlockSpec((1, tn), lambda j: (0, j))],
        out_specs=pl.BlockSpec((1, tn), lambda j: (0, j)),
        compiler_params=pltpu.CompilerParams(
            dimension_semantics=("arbitrary",), vmem_limit_bytes=VMEM_LIMIT_BYTES),
        name="mod",
    )(c_col, w_mod, b_mod)


def _inproj_kernel(x_ref, scale_ref, shift_ref, cos_ref, sin_ref, w_ref, o_ref, qt_ref, vt_ref, u_sc,
                   *, slabs, tq, tk):
    j = pl.program_id(1)
    tm = x_ref.shape[0]

    @pl.when(j == 0)
    def _():
        u_sc[...] = (x_ref[...] * (1.0 + scale_ref[...]) + shift_ref[...]).astype(BF16)

    def project(epilogue, transposed=None):
        for c in range(slabs * LANES // MXU_N):
            acc = jnp.dot(u_sc[...], w_ref[:, c * MXU_N:(c + 1) * MXU_N],
                          preferred_element_type=F32)
            for h in range(MXU_N // LANES):
                s = c * (MXU_N // LANES) + h
                r = epilogue(acc[:, h * LANES:(h + 1) * LANES])
                o_ref[s] = r.astype(BF16)
                if transposed is not None:
                    transposed(s, r)

    def rope(qscale):
        lane = lax.broadcasted_iota(jnp.int32, (tm, LANES), 1)
        first_half = (lane % D_HEAD) < (D_HEAD // 2)

        def apply(t):
            partner = jnp.where(first_half,
                                pltpu.roll(t, LANES - D_HEAD // 2, axis=1),
                                pltpu.roll(t, D_HEAD // 2, axis=1))
            r = t * cos_ref[...] + partner * sin_ref[...]
            return r if qscale is None else r * qscale
        return apply

    def store_qt(s, r):
        for qblk in range(tm // tq):
            qt_ref[s, qblk] = r[qblk * tq:(qblk + 1) * tq, :].T.astype(BF16)

    def store_vt(s, r):
        for cblk in range(tm // tk):
            vt_ref[s, cblk, :LANES, :] = r[cblk * tk:(cblk + 1) * tk, :].T.astype(BF16)
            vt_ref[s, cblk, LANES:, :] = jnp.ones((ONES_ROWS, tk), BF16)

    @pl.when(j == 0)
    def _():
        project(rope(D_HEAD ** -0.5 * LOG2_E), store_qt)

    @pl.when(j == 1)
    def _():
        project(rope(None))

    @pl.when(j == 2)
    def _():
        project(lambda t: t, store_vt)

    @pl.when(j >= 3)
    def _():
        project(lambda t: t)


def _in_proj(x2, scale, shift, cos_t, sin_t, w_in_bf, attn_width, tq, tk, tm=1024):
    s_len, d = x2.shape
    n = w_in_bf.shape[1]
    tn = attn_width
    slabs = tn // LANES
    kern = functools.partial(_inproj_kernel, slabs=slabs, tq=tq, tk=tk)
    return pl.pallas_call(
        kern,
        out_shape=(jax.ShapeDtypeStruct((n // LANES, s_len, LANES), BF16),
                   jax.ShapeDtypeStruct((slabs, s_len // tq, LANES, tq), BF16),
                   jax.ShapeDtypeStruct((slabs, s_len // tk, LANES + ONES_ROWS, tk), BF16)),
        grid=(s_len // tm, n // tn),
        in_specs=[pl.BlockSpec((tm, d), lambda i, j: (i, 0)),
                  pl.BlockSpec((1, d), lambda i, j: (0, 0)),
                  pl.BlockSpec((1, d), lambda i, j: (0, 0)),
                  pl.BlockSpec((tm, LANES), lambda i, j: (i, 0)),
                  pl.BlockSpec((tm, LANES), lambda i, j: (i, 0)),
                  pl.BlockSpec((d, tn), lambda i, j: (0, j))],
        out_specs=(pl.BlockSpec((slabs, tm, LANES), lambda i, j: (j, i, 0)),
                   pl.BlockSpec((slabs, tm // tq, LANES, tq), lambda i, j: (0, i, 0, 0)),
                   pl.BlockSpec((slabs, tm // tk, LANES + ONES_ROWS, tk), lambda i, j: (0, i, 0, 0))),
        scratch_shapes=[pltpu.VMEM((tm, d), BF16)],
        compiler_params=pltpu.CompilerParams(
            dimension_semantics=("arbitrary", "arbitrary"), vmem_limit_bytes=VMEM_LIMIT_BYTES),
        name="in_proj",
    )(x2, scale, shift, cos_t, sin_t, w_in_bf)


def _attn_kernel(lq1_ref, lk1_ref, lq2_ref, lk2_ref, gain_ref, qt_ref, qtn_ref, k_ref, vt_ref, z_ref,
                 o_ref, acc_sc, s_sc, mx_sc, qm_sc, *, tq, tk, lambda_init):
    g = pl.program_id(1)

    def one_tile(t, carry):
        i = QT_PER_STEP * g + t
        drow = lax.broadcasted_iota(jnp.int32, (LANES, tq), 0)
        zero = jnp.zeros((LANES, tq), BF16)
        q_nxt = jnp.where(t + 1 < QT_PER_STEP,
                          qt_ref[0, jnp.minimum(t + 1, QT_PER_STEP - 1)], qtn_ref[0, 0])
        for which, qv in enumerate((qt_ref[0, t], q_nxt)):
            qm_sc[which, 0] = jnp.where(drow < D_HEAD, qv, zero)
            qm_sc[which, 1] = jnp.where(drow >= D_HEAD, qv, zero)
        acc_sc[...] = jnp.zeros(acc_sc.shape, F32)
        units = [(mp, c) for mp in range(2) for c in range(tq // MXU_N)]
        cols = lambda c: slice(c * MXU_N, (c + 1) * MXU_N)

        def qk_unit(j, slot, mp, c, which=0):
            k = k_ref[0, pl.ds(pl.multiple_of(j * tk, tk), tk), :]
            s = jnp.dot(k, qm_sc[which, mp, :, cols(c)], preferred_element_type=F32)
            s_sc[slot, mp, :, cols(c)] = s
            mx_sc[slot, mp, :, cols(c)] = jnp.max(s, axis=0, keepdims=True)

        def softmax_pv_unit(j, slot, mp, c, m_old, masked):
            s = s_sc[slot, mp, :, cols(c)]
            if masked:
                key = lax.broadcasted_iota(jnp.int32, s.shape, 0)
                qry = lax.broadcasted_iota(jnp.int32, s.shape, 1) + c * MXU_N
                s = jnp.where(key <= qry, s, -jnp.inf)
                m_cur = jnp.max(s, axis=0, keepdims=True)
            else:
                m_cur = mx_sc[slot, mp, :, cols(c)]
            m_new = jnp.maximum(m_old, m_cur)
            p = jnp.exp2((s - m_new).astype(BF16))
            alpha = jnp.exp2(m_old - m_new)
            acc_sc[mp, :, cols(c)] = alpha * acc_sc[mp, :, cols(c)] + jnp.dot(
                vt_ref[0, j], p, preferred_element_type=F32)
            return m_new

        def stage(j_next, slot_next, j, slot, m, masked):
            m_out = []
            for u, (mp, c) in enumerate(units):
                qk_unit(j_next, slot_next, mp, c)
                m_out.append(softmax_pv_unit(j, slot, mp, c, m[u], masked))
            return tuple(m_out)

        def last_stage(slot, m):
            for u, (mp, c) in enumerate(units):
                softmax_pv_unit(i, slot, mp, c, m[u], True)
                qk_unit(0, 0, mp, c, which=1)

        def pair(jj, m):
            j = 2 * jj
            m = stage(j + 1, 1, j, 0, m, False)
            return stage(j + 2, 0, j + 1, 1, m, False)

        @pl.when(i == 0)
        def _():
            for mp, c in units:
                qk_unit(0, 0, mp, c)

        neg = jnp.full((1, MXU_N), -jnp.inf, F32)
        quad = lambda q, m: pair(2 * q + 1, pair(2 * q, m))
        octo = lambda o, m: quad(2 * o + 1, quad(2 * o, m))
        m = lax.fori_loop(0, i // 16, lambda x, m: octo(2 * x + 1, octo(2 * x, m)), (neg,) * len(units))
        m = lax.fori_loop(2 * (i // 16), i // 8, octo, m)
        m = lax.fori_loop(2 * (i // 8), i // 4, quad, m)
        m = lax.fori_loop(2 * (i // 4), i // 2, pair, m)

        def finalize():
            lam = (jnp.exp(jnp.sum(lq1_ref[...] * lk1_ref[...], axis=1, keepdims=True))
                   - jnp.exp(jnp.sum(lq2_ref[...] * lk2_ref[...], axis=1, keepdims=True))
                   + lambda_init)
            a0, a1 = acc_sc[0], acc_sc[1]
            ot = (a0[:D_VALUE] * (1.0 / a0[D_VALUE:D_VALUE + 1])
                  - lam * (a1[:D_VALUE] * (1.0 / a1[D_VALUE:D_VALUE + 1])))
            o = ot.T
            o = o * lax.rsqrt(jnp.mean(o * o, axis=1, keepdims=True) + NORM_EPS)
            o = o * gain_ref[...] * (1.0 - lambda_init)
            rows = pl.ds(pl.multiple_of(t * tq, tq), tq)
            z = z_ref[0, rows, :].astype(F32)
            o_ref[rows, :] = (o * (z * _sigmoid(z))).astype(o_ref.dtype)

        @pl.when(i % 2 == 1)
        def _():
            last_stage(1, stage(i, 1, i - 1, 0, m, False))
            finalize()

        @pl.when(i % 2 == 0)
        def _():
            last_stage(0, m)
            finalize()
        return carry

    lax.fori_loop(0, QT_PER_STEP, one_tile, 0)


def _attention(proj, q_t, v_t, lq1, lk1, lq2, lk2, gain, n_heads, lambda_init, tq, tk):
    n_slabs, s_len, _ = proj.shape
    assert tq == tk, "the diagonal tile mask assumes square tiles"
    n_q = s_len // tq
    kern = functools.partial(_attn_kernel, tq=tq, tk=tk, lambda_init=lambda_init)
    vec = lambda n: pl.BlockSpec((1, n), lambda h, g: (0, 0))
    return pl.pallas_call(
        kern,
        out_shape=jax.ShapeDtypeStruct((s_len, n_heads * D_VALUE), BF16),
        grid=(n_heads, n_q // QT_PER_STEP),
        in_specs=[vec(D_HEAD), vec(D_HEAD), vec(D_HEAD), vec(D_HEAD), vec(D_VALUE),
                  pl.BlockSpec((1, QT_PER_STEP, LANES, tq), lambda h, g: (h, g, 0, 0)),
                  pl.BlockSpec((1, 1, LANES, tq),
                               lambda h, g: (h, jnp.minimum(QT_PER_STEP * (g + 1), n_q - 1), 0, 0)),
                  pl.BlockSpec((1, s_len, LANES), lambda h, g: (n_heads + h, 0, 0)),
                  pl.BlockSpec((1, s_len // tk, D_VALUE + ONES_ROWS, tk),
                               lambda h, g: (h, 0, 0, 0)),
                  pl.BlockSpec((1, QT_PER_STEP * tq, LANES),
                               lambda h, g: (3 * n_heads + h, g, 0))],
        out_specs=pl.BlockSpec((QT_PER_STEP * tq, D_VALUE), lambda h, g: (g, h)),
        scratch_shapes=[pltpu.VMEM((2, D_VALUE + ONES_ROWS, tq), F32),
                        pltpu.VMEM((2, 2, tk, tq), F32),
                        pltpu.VMEM((2, 2, 1, tq), F32),
                        pltpu.VMEM((2, 2, LANES, tq), BF16)],
        compiler_params=pltpu.CompilerParams(
            dimension_semantics=("arbitrary", "arbitrary"), vmem_limit_bytes=VMEM_LIMIT_BYTES),
        name="attn",
    )(lq1, lk1, lq2, lk2, gain, q_t, q_t, proj, v_t, proj)


def _out_kernel(ya_ref, cb_ref, cc_ref, cx_ref, zb_ref, hc_ref, hx_ref, ga_ref, gb_ref, x_ref,
                gate_ref, cw_ref, pa_ref, pb_ref, wo_ref, lng_ref, lnb_ref, o_ref,
                *, tm, conv_slabs, gate_slabs, alpha):
    i = pl.program_id(0)
    row = lax.broadcasted_iota(jnp.int32, (tm, LANES), 0)
    have_prev = i > 0
    yb = []
    for s in range(conv_slabs):
        u = cc_ref[s].astype(F32) * cx_ref[s].astype(F32)
        prev = jnp.where(have_prev, hc_ref[s].astype(F32) * hx_ref[s].astype(F32), 0.0)
        p1 = prev[SUBLANES - 1:SUBLANES, :]
        p2 = prev[SUBLANES - 2:SUBLANES - 1, :]
        u1 = jnp.where(row == 0, p1, pltpu.roll(u, 1, axis=0))
        u2 = jnp.where(row == 0, p2, jnp.where(row == 1, p1, pltpu.roll(u, 2, axis=0)))
        w = cw_ref[:, s * LANES:(s + 1) * LANES]
        conv = w[0:1, :] * u2 + w[1:2, :] * u1 + w[2:3, :] * u
        z = zb_ref[s].astype(F32)
        yb.append((cb_ref[s].astype(F32) * conv * (z * _sigmoid(z))).astype(BF16))
    y_b = jnp.concatenate(yb, axis=1)
    ha = jnp.dot(ya_ref[...], pa_ref[...], preferred_element_type=F32)
    hb = jnp.dot(y_b, pb_ref[...], preferred_element_type=F32)
    g_a = jnp.concatenate([ga_ref[s] for s in range(gate_slabs)], axis=1).astype(F32)
    g_b = jnp.concatenate([gb_ref[s] for s in range(gate_slabs)], axis=1).astype(F32)
    h = (_sigmoid(g_a) * ha + _sigmoid(g_b) * hb).astype(BF16)
    out = jnp.dot(h, wo_ref[...], preferred_element_type=F32)
    r = alpha * x_ref[...] + gate_ref[...] * out
    mu = jnp.mean(r, axis=1, keepdims=True)
    rc = r - mu
    var = jnp.mean(rc * rc, axis=1, keepdims=True)
    o_ref[...] = rc * lax.rsqrt(var + NORM_EPS) * lng_ref[...] + lnb_ref[...]


def _out_stage(y_a, proj, x2, gate, conv_w, pa_bf, pb_bf, wo_bf, ln_g, ln_b, alpha,
               seg_off, tm=256):
    s_len, d = x2.shape
    conv_w_width = conv_w.shape[1]
    conv_slabs = conv_w_width // LANES
    gate_slabs = d // LANES
    off_cb, off_cc, off_cx, off_zb, off_ga, off_gb = [o // LANES for o in seg_off]
    kern = functools.partial(_out_kernel, tm=tm, conv_slabs=conv_slabs, gate_slabs=gate_slabs,
                             alpha=alpha)
    slab = lambda nsl, off: pl.BlockSpec((nsl, tm, LANES), lambda i: (off // nsl, i, 0))
    halo = lambda off: pl.BlockSpec(
        (conv_slabs, SUBLANES, LANES),
        lambda i: (off // conv_slabs, jnp.maximum(i * (tm // SUBLANES) - 1, 0), 0))
    const = lambda shape: pl.BlockSpec(shape, lambda i: (0, 0), pipeline_mode=pl.Buffered(1))
    return pl.pallas_call(
        kern,
        out_shape=jax.ShapeDtypeStruct((s_len, d), F32),
        grid=(s_len // tm,),
        in_specs=[pl.BlockSpec((tm, y_a.shape[1]), lambda i: (i, 0)),
                  slab(conv_slabs, off_cb), slab(conv_slabs, off_cc), slab(conv_slabs, off_cx),
                  slab(conv_slabs, off_zb), halo(off_cc), halo(off_cx),
                  slab(gate_slabs, off_ga), slab(gate_slabs, off_gb),
                  pl.BlockSpec((tm, d), lambda i: (i, 0)),
                  const((1, d)), const((CONV_SIZE, conv_w_width)),
                  const(pa_bf.shape), const(pb_bf.shape), const(wo_bf.shape),
                  const((1, d)), const((1, d))],
        out_specs=pl.BlockSpec((tm, d), lambda i: (i, 0)),
        compiler_params=pltpu.CompilerParams(
            dimension_semantics=("arbitrary",), vmem_limit_bytes=VMEM_LIMIT_BYTES),
        name="out_stage",
    )(y_a, proj, proj, proj, proj, proj, proj, proj, proj, x2, gate, conv_w, pa_bf, pb_bf, wo_bf,
      ln_g, ln_b)


def _rope_tables(positions_1d):
    lane = jnp.arange(LANES)
    inv_freq = ROPE_THETA ** (-(2 * (lane % (D_HEAD // 2))).astype(F32) / D_HEAD)
    sign = jnp.where((lane % D_HEAD) < D_HEAD // 2, -1.0, 1.0).astype(F32)
    ang = positions_1d.astype(F32)[:, None] * inv_freq
    return jnp.cos(ang), jnp.sin(ang) * sign


def kernel(x, c, positions, w_mod, b_mod, w_in, lambda_q1, lambda_k1, lambda_q2, lambda_k2,
           subln_gain, conv_w, w_proj_a, w_proj_b, w_out, ln_gain, ln_bias):
    batch, s_len, d = x.shape
    depth = w_mod.shape[0]
    attn_width = w_proj_a.shape[1]
    conv_width = w_proj_b.shape[1]
    n_heads = attn_width // D_VALUE
    alpha = (2.0 * depth) ** 0.25
    seg_off = (4 * attn_width, 4 * attn_width + conv_width, 4 * attn_width + 2 * conv_width,
               4 * attn_width + 3 * conv_width, 4 * attn_width + 4 * conv_width,
               4 * attn_width + 4 * conv_width + d)
    outs = []
    for b in range(batch):
        h = x[b]
        cos_t, sin_t = _rope_tables(positions[b])
        c_col = c[b].reshape(d, 1)
        for l in range(depth):
            mod = _mod(c_col, w_mod[l], b_mod[l].reshape(1, 3 * d))
            shift, scale, gate = mod[:, :d], mod[:, d:2 * d], mod[:, 2 * d:]
            proj, q_t, v_t = _in_proj(h, scale, shift, cos_t, sin_t, w_in[l].astype(BF16),
                                      attn_width, ATTN_TQ, ATTN_TK)
            y_a = _attention(proj, q_t, v_t,
                             lambda_q1[l].reshape(1, D_HEAD), lambda_k1[l].reshape(1, D_HEAD),
                             lambda_q2[l].reshape(1, D_HEAD), lambda_k2[l].reshape(1, D_HEAD),
                             subln_gain[l].reshape(1, D_VALUE), n_heads, _lambda_init(l),
                             ATTN_TQ, ATTN_TK)
            h = _out_stage(y_a, proj, h, gate, conv_w[l], w_proj_a[l].astype(BF16),
                           w_proj_b[l].astype(BF16), w_out[l].astype(BF16),
                           ln_gain[l].reshape(1, d), ln_bias[l].reshape(1, d), alpha, seg_off)
        outs.append(h)
    return outs[0][None] if batch == 1 else jnp.stack(outs, axis=0)
```

```python
import functools
import math

import jax
import jax.numpy as jnp
from jax import lax
from jax.experimental import pallas as pl
from jax.experimental.pallas import tpu as pltpu

D_HEAD = 64
D_VALUE = 2 * D_HEAD
CONV_SIZE = 3
ROPE_THETA = 10000.0
NORM_EPS = 1e-5

LANES = 128
SUBLANES = 8
MXU_N = 256
V7X_VMEM_BYTES = 64 * 1024 * 1024
VMEM_LIMIT_BYTES = V7X_VMEM_BYTES * 7 // 8
ATTN_TQ = 512
ATTN_TK = 512
QT_PER_STEP = 2
ONES_ROWS = 16
LOG2_E = math.log2(math.e)

F32 = jnp.float32
BF16 = jnp.bfloat16


def _lambda_init(layer):
    return 0.8 - 0.6 * math.exp(-0.3 * layer)


def _sigmoid(z):
    return 1.0 / (1.0 + jnp.exp(-z))


def _mod_kernel(c_ref, w_ref, b_ref, o_ref):
    o_ref[...] = jnp.sum(w_ref[...] * c_ref[...], axis=0, keepdims=True) + b_ref[...]


def _mod(c_col, w_mod, b_mod, tn=512):
    d, n = w_mod.shape
    return pl.pallas_call(
        _mod_kernel,
        out_shape=jax.ShapeDtypeStruct((1, n), F32),
        grid=(n // tn,),
        in_specs=[pl.BlockSpec((d, 1), lambda j: (0, 0)),
                  pl.BlockSpec((d, tn), lambda j: (0, j)),
                  pl.BlockSpec((1, tn), lambda j: (0, j))],
        out_specs=pl.BlockSpec((1, tn), lambda j: (0, j)),
        compiler_params=pltpu.CompilerParams(
            dimension_semantics=("arbitrary",), vmem_limit_bytes=VMEM_LIMIT_BYTES),
        name="mod",
    )(c_col, w_mod, b_mod)


def _inproj_kernel(x_ref, scale_ref, shift_ref, cos_ref, sin_ref, w_ref, o_ref, qt_ref, vt_ref, u_sc,
                   *, slabs, tq, tk):
    j = pl.program_id(1)
    tm = x_ref.shape[0]

    @pl.when(j == 0)
    def _():
        u_sc[...] = (x_ref[...] * (1.0 + scale_ref[...]) + shift_ref[...]).astype(BF16)

    def project(epilogue, transposed=None):
        for c in range(slabs * LANES // MXU_N):
            acc = jnp.dot(u_sc[...], w_ref[:, c * MXU_N:(c + 1) * MXU_N],
                          preferred_element_type=F32)
            for h in range(MXU_N // LANES):
                s = c * (MXU_N // LANES) + h
                r = epilogue(acc[:, h * LANES:(h + 1) * LANES])
                o_ref[s] = r.astype(BF16)
                if transposed is not None:
                    transposed(s, r)

    def rope(qscale):
        lane = lax.broadcasted_iota(jnp.int32, (tm, LANES), 1)
        first_half = (lane % D_HEAD) < (D_HEAD // 2)

        def apply(t):
            partner = jnp.where(first_half,
                                pltpu.roll(t, LANES - D_HEAD // 2, axis=1),
                                pltpu.roll(t, D_HEAD // 2, axis=1))
            r = t * cos_ref[...] + partner * sin_ref[...]
            return r if qscale is None else r * qscale
        return apply

    def store_qt(s, r):
        for qblk in range(tm // tq):
            qt_ref[s, qblk] = r[qblk * tq:(qblk + 1) * tq, :].T.astype(BF16)

    def store_vt(s, r):
        for cblk in range(tm // tk):
            vt_ref[s, cblk, :LANES, :] = r[cblk * tk:(cblk + 1) * tk, :].T.astype(BF16)
            vt_ref[s, cblk, LANES:, :] = jnp.ones((ONES_ROWS, tk), BF16)

    @pl.when(j == 0)
    def _():
        project(rope(D_HEAD ** -0.5 * LOG2_E), store_qt)

    @pl.when(j == 1)
    def _():
        project(rope(None))

    @pl.when(j == 2)
    def _():
        project(lambda t: t, store_vt)

    @pl.when(j >= 3)
    def _():
        project(lambda t: t)


def _in_proj(x2, scale, shift, cos_t, sin_t, w_in_bf, attn_width, tq, tk, tm=1024):
    s_len, d = x2.shape
    n = w_in_bf.shape[1]
    tn = attn_width
    slabs = tn // LANES
    kern = functools.partial(_inproj_kernel, slabs=slabs, tq=tq, tk=tk)
    return pl.pallas_call(
        kern,
        out_shape=(jax.ShapeDtypeStruct((n // LANES, s_len, LANES), BF16),
                   jax.ShapeDtypeStruct((slabs, s_len // tq, LANES, tq), BF16),
                   jax.ShapeDtypeStruct((slabs, s_len // tk, LANES + ONES_ROWS, tk), BF16)),
        grid=(s_len // tm, n // tn),
        in_specs=[pl.BlockSpec((tm, d), lambda i, j: (i, 0)),
                  pl.BlockSpec((1, d), lambda i, j: (0, 0)),
                  pl.BlockSpec((1, d), lambda i, j: (0, 0)),
                  pl.BlockSpec((tm, LANES), lambda i, j: (i, 0)),
                  pl.BlockSpec((tm, LANES), lambda i, j: (i, 0)),
                  pl.BlockSpec((d, tn), lambda i, j: (0, j))],
        out_specs=(pl.BlockSpec((slabs, tm, LANES), lambda i, j: (j, i, 0)),
                   pl.BlockSpec((slabs, tm // tq, LANES, tq), lambda i, j: (0, i, 0, 0)),
                   pl.BlockSpec((slabs, tm // tk, LANES + ONES_ROWS, tk), lambda i, j: (0, i, 0, 0))),
        scratch_shapes=[pltpu.VMEM((tm, d), BF16)],
        compiler_params=pltpu.CompilerParams(
            dimension_semantics=("arbitrary", "arbitrary"), vmem_limit_bytes=VMEM_LIMIT_BYTES),
        name="in_proj",
    )(x2, scale, shift, cos_t, sin_t, w_in_bf)


def _attn_kernel(lq1_ref, lk1_ref, lq2_ref, lk2_ref, gain_ref, qt_ref, qtn_ref, k_ref, vt_ref, z_ref,
                 o_ref, acc_sc, s_sc, mx_sc, qm_sc, *, tq, tk, lambda_init):
    g = pl.program_id(1)

    def one_tile(t, carry):
        i = QT_PER_STEP * g + t
        drow = lax.broadcasted_iota(jnp.int32, (LANES, tq), 0)
        zero = jnp.zeros((LANES, tq), BF16)
        q_nxt = jnp.where(t + 1 < QT_PER_STEP,
                          qt_ref[0, jnp.minimum(t + 1, QT_PER_STEP - 1)], qtn_ref[0, 0])
        for which, qv in enumerate((qt_ref[0, t], q_nxt)):
            qm_sc[which, 0] = jnp.where(drow < D_HEAD, qv, zero)
            qm_sc[which, 1] = jnp.where(drow >= D_HEAD, qv, zero)
        acc_sc[...] = jnp.zeros(acc_sc.shape, F32)
        units = [(mp, c) for mp in range(2) for c in range(tq // MXU_N)]
        cols = lambda c: slice(c * MXU_N, (c + 1) * MXU_N)

        def qk_unit(j, slot, mp, c, which=0):
            k = k_ref[0, pl.ds(pl.multiple_of(j * tk, tk), tk), :]
            s = jnp.dot(k, qm_sc[which, mp, :, cols(c)], preferred_element_type=F32)
            s_sc[slot, mp, :, cols(c)] = s
            mx_sc[slot, mp, :, cols(c)] = jnp.max(s, axis=0, keepdims=True)

        def softmax_pv_unit(j, slot, mp, c, m_old, masked):
            s = s_sc[slot, mp, :, cols(c)]
            if masked:
                key = lax.broadcasted_iota(jnp.int32, s.shape, 0)
                qry = lax.broadcasted_iota(jnp.int32, s.shape, 1) + c * MXU_N
                s = jnp.where(key <= qry, s, -jnp.inf)
                m_cur = jnp.max(s, axis=0, keepdims=True)
            else:
                m_cur = mx_sc[slot, mp, :, cols(c)]
            m_new = jnp.maximum(m_old, m_cur)
            p = jnp.exp2((s - m_new).astype(BF16))
            alpha = jnp.exp2(m_old - m_new)
            acc_sc[mp, :, cols(c)] = alpha * acc_sc[mp, :, cols(c)] + jnp.dot(
                vt_ref[0, j], p, preferred_element_type=F32)
            return m_new

        def stage(j_next, slot_next, j, slot, m, masked):
            m_out = []
            for u, (mp, c) in enumerate(units):
                qk_unit(j_next, slot_next, mp, c)
                m_out.append(softmax_pv_unit(j, slot, mp, c, m[u], masked))
            return tuple(m_out)

        def last_stage(slot, m):
            for u, (mp, c) in enumerate(units):
                softmax_pv_unit(i, slot, mp, c, m[u], True)
                qk_unit(0, 0, mp, c, which=1)

        def pair(jj, m):
            j = 2 * jj
            m = stage(j + 1, 1, j, 0, m, False)
            return stage(j + 2, 0, j + 1, 1, m, False)

        @pl.when(i == 0)
        def _():
            for mp, c in units:
                qk_unit(0, 0, mp, c)

        neg = jnp.full((1, MXU_N), -jnp.inf, F32)
        quad = lambda q, m: pair(2 * q + 1, pair(2 * q, m))
        octo = lambda o, m: quad(2 * o + 1, quad(2 * o, m))
        m = lax.fori_loop(0, i // 16, lambda x, m: octo(2 * x + 1, octo(2 * x, m)), (neg,) * len(units))
        m = lax.fori_loop(2 * (i // 16), i // 8, octo, m)
        m = lax.fori_loop(2 * (i // 8), i // 4, quad, m)
        m = lax.fori_loop(2 * (i // 4), i // 2, pair, m)

        def finalize():
            lam = (jnp.exp(jnp.sum(lq1_ref[...] * lk1_ref[...], axis=1, keepdims=True))
                   - jnp.exp(jnp.sum(lq2_ref[...] * lk2_ref[...], axis=1, keepdims=True))
                   + lambda_init)
            a0, a1 = acc_sc[0], acc_sc[1]
            ot = (a0[:D_VALUE] * (1.0 / a0[D_VALUE:D_VALUE + 1])
                  - lam * (a1[:D_VALUE] * (1.0 / a1[D_VALUE:D_VALUE + 1])))
            o = ot.T
            o = o * lax.rsqrt(jnp.mean(o * o, axis=1, keepdims=True) + NORM_EPS)
            o = o * gain_ref[...] * (1.0 - lambda_init)
            rows = pl.ds(pl.multiple_of(t * tq, tq), tq)
            z = z_ref[0, rows, :].astype(F32)
            o_ref[rows, :] = (o * (z * _sigmoid(z))).astype(o_ref.dtype)

        @pl.when(i % 2 == 1)
        def _():
            last_stage(1, stage(i, 1, i - 1, 0, m, False))
            finalize()

        @pl.when(i % 2 == 0)
        def _():
            last_stage(0, m)
            finalize()
        return carry

    lax.fori_loop(0, QT_PER_STEP, one_tile, 0)


def _attention(proj, q_t, v_t, lq1, lk1, lq2, lk2, gain, n_heads, lambda_init, tq, tk):
    n_slabs, s_len, _ = proj.shape
    assert tq == tk, "the diagonal tile mask assumes square tiles"
    n_q = s_len // tq
    kern = functools.partial(_attn_kernel, tq=tq, tk=tk, lambda_init=lambda_init)
    vec = lambda n: pl.BlockSpec((1, n), lambda h, g: (0, 0))
    return pl.pallas_call(
        kern,
        out_shape=jax.ShapeDtypeStruct((s_len, n_heads * D_VALUE), BF16),
        grid=(n_heads, n_q // QT_PER_STEP),
        in_specs=[vec(D_HEAD), vec(D_HEAD), vec(D_HEAD), vec(D_HEAD), vec(D_VALUE),
                  pl.BlockSpec((1, QT_PER_STEP, LANES, tq), lambda h, g: (h, g, 0, 0)),
                  pl.BlockSpec((1, 1, LANES, tq),
                               lambda h, g: (h, jnp.minimum(QT_PER_STEP * (g + 1), n_q - 1), 0, 0)),
                  pl.BlockSpec((1, s_len, LANES), lambda h, g: (n_heads + h, 0, 0)),
                  pl.BlockSpec((1, s_len // tk, D_VALUE + ONES_ROWS, tk),
                               lambda h, g: (h, 0, 0, 0)),
                  pl.BlockSpec((1, QT_PER_STEP * tq, LANES),
                               lambda h, g: (3 * n_heads + h, g, 0))],
        out_specs=pl.BlockSpec((QT_PER_STEP * tq, D_VALUE), lambda h, g: (g, h)),
        scratch_shapes=[pltpu.VMEM((2, D_VALUE + ONES_ROWS, tq), F32),
                        pltpu.VMEM((2, 2, tk, tq), F32),
                        pltpu.VMEM((2, 2, 1, tq), F32),
                        pltpu.VMEM((2, 2, LANES, tq), BF16)],
        compiler_params=pltpu.CompilerParams(
            dimension_semantics=("arbitrary", "arbitrary"), vmem_limit_bytes=VMEM_LIMIT_BYTES),
        name="attn",
    )(lq1, lk1, lq2, lk2, gain, q_t, q_t, proj, v_t, proj)


def _out_kernel(ya_ref, cb_ref, cc_ref, cx_ref, zb_ref, hc_ref, hx_ref, ga_ref, gb_ref, x_ref,
                gate_ref, cw_ref, pa_ref, pb_ref, wo_ref, lng_ref, lnb_ref, o_ref,
                *, tm, conv_slabs, gate_slabs, alpha):
    i = pl.program_id(0)
    row = lax.broadcasted_iota(jnp.int32, (tm, LANES), 0)
    have_prev = i > 0
    ya = ya_ref[...]
    assert gate_slabs * LANES // MXU_N == conv_slabs
    has, yb, zeros = [], [], []
    lag = 2
    for s in range(conv_slabs):
        lhs = ya if s < lag else jnp.concatenate([ya[:, :LANES] + zeros[s - lag], ya[:, LANES:]], axis=1)
        has.append(jnp.dot(lhs, pa_ref[:, s * MXU_N:(s + 1) * MXU_N], preferred_element_type=F32))
        u = cc_ref[s].astype(F32) * cx_ref[s].astype(F32)
        prev = jnp.where(have_prev, hc_ref[s].astype(F32) * hx_ref[s].astype(F32), 0.0)
        p1 = prev[SUBLANES - 1:SUBLANES, :]
        p2 = prev[SUBLANES - 2:SUBLANES - 1, :]
        u1 = jnp.where(row == 0, p1, pltpu.roll(u, 1, axis=0))
        u2 = jnp.where(row == 0, p2, jnp.where(row == 1, p1, pltpu.roll(u, 2, axis=0)))
        w = cw_ref[:, s * LANES:(s + 1) * LANES]
        conv = w[0:1, :] * u2 + w[1:2, :] * u1 + w[2:3, :] * u
        z = zb_ref[s].astype(F32)
        yb_f32 = cb_ref[s].astype(F32) * conv * (z * _sigmoid(z))
        yb.append(yb_f32.astype(BF16))
        bits = pltpu.bitcast(yb_f32, jnp.uint32)
        zeros.append(pltpu.bitcast((bits >> 16) >> 16, F32).astype(BF16))
    y_b = jnp.concatenate(yb, axis=1)
    ha = jnp.concatenate(has, axis=1)
    hb = jnp.dot(y_b, pb_ref[...], preferred_element_type=F32)
    g_a = jnp.concatenate([ga_ref[s] for s in range(gate_slabs)], axis=1).astype(F32)
    g_b = jnp.concatenate([gb_ref[s] for s in range(gate_slabs)], axis=1).astype(F32)
    h = (_sigmoid(g_a) * ha + _sigmoid(g_b) * hb).astype(BF16)
    out = jnp.dot(h, wo_ref[...], preferred_element_type=F32)
    r = alpha * x_ref[...] + gate_ref[...] * out
    mu = jnp.mean(r, axis=1, keepdims=True)
    rc = r - mu
    var = jnp.mean(rc * rc, axis=1, keepdims=True)
    o_ref[...] = rc * lax.rsqrt(var + NORM_EPS) * lng_ref[...] + lnb_ref[...]


def _out_stage(y_a, proj, x2, gate, conv_w, pa_bf, pb_bf, wo_bf, ln_g, ln_b, alpha,
               seg_off, tm=256):
    s_len, d = x2.shape
    conv_w_width = conv_w.shape[1]
    conv_slabs = conv_w_width // LANES
    gate_slabs = d // LANES
    off_cb, off_cc, off_cx, off_zb, off_ga, off_gb = [o // LANES for o in seg_off]
    kern = functools.partial(_out_kernel, tm=tm, conv_slabs=conv_slabs, gate_slabs=gate_slabs,
                             alpha=alpha)
    slab = lambda nsl, off: pl.BlockSpec((nsl, tm, LANES), lambda i: (off // nsl, i, 0))
    halo = lambda off: pl.BlockSpec(
        (conv_slabs, SUBLANES, LANES),
        lambda i: (off // conv_slabs, jnp.maximum(i * (tm // SUBLANES) - 1, 0), 0))
    const = lambda shape: pl.BlockSpec(shape, lambda i: (0, 0), pipeline_mode=pl.Buffered(1))
    return pl.pallas_call(
        kern,
        out_shape=jax.ShapeDtypeStruct((s_len, d), F32),
        grid=(s_len // tm,),
        in_specs=[pl.BlockSpec((tm, y_a.shape[1]), lambda i: (i, 0)),
                  slab(conv_slabs, off_cb), slab(conv_slabs, off_cc), slab(conv_slabs, off_cx),
                  slab(conv_slabs, off_zb), halo(off_cc), halo(off_cx),
                  slab(gate_slabs, off_ga), slab(gate_slabs, off_gb),
                  pl.BlockSpec((tm, d), lambda i: (i, 0)),
                  const((1, d)), const((CONV_SIZE, conv_w_width)),
                  const(pa_bf.shape), const(pb_bf.shape), const(wo_bf.shape),
                  const((1, d)), const((1, d))],
        out_specs=pl.BlockSpec((tm, d), lambda i: (i, 0)),
        compiler_params=pltpu.CompilerParams(
            dimension_semantics=("arbitrary",), vmem_limit_bytes=VMEM_LIMIT_BYTES),
        name="out_stage",
    )(y_a, proj, proj, proj, proj, proj, proj, proj, proj, x2, gate, conv_w, pa_bf, pb_bf, wo_bf,
      ln_g, ln_b)


def _rope_tables(positions_1d):
    lane = jnp.arange(LANES)
    inv_freq = ROPE_THETA ** (-(2 * (lane % (D_HEAD // 2))).astype(F32) / D_HEAD)
    sign = jnp.where((lane % D_HEAD) < D_HEAD // 2, -1.0, 1.0).astype(F32)
    ang = positions_1d.astype(F32)[:, None] * inv_freq
    return jnp.cos(ang), jnp.sin(ang) * sign


def kernel(x, c, positions, w_mod, b_mod, w_in, lambda_q1, lambda_k1, lambda_q2, lambda_k2,
           subln_gain, conv_w, w_proj_a, w_proj_b, w_out, ln_gain, ln_bias):
    batch, s_len, d = x.shape
    depth = w_mod.shape[0]
    attn_width = w_proj_a.shape[1]
    conv_width = w_proj_b.shape[1]
    n_heads = attn_width // D_VALUE
    alpha = (2.0 * depth) ** 0.25
    seg_off = (4 * attn_width, 4 * attn_width + conv_width, 4 * attn_width + 2 * conv_width,
               4 * attn_width + 3 * conv_width, 4 * attn_width + 4 * conv_width,
               4 * attn_width + 4 * conv_width + d)
    outs = []
    for b in range(batch):
        h = x[b]
        cos_t, sin_t = _rope_tables(positions[b])
        c_col = c[b].reshape(d, 1)
        for l in range(depth):
            mod = _mod(c_col, w_mod[l], b_mod[l].reshape(1, 3 * d))
            shift, scale, gate = mod[:, :d], mod[:, d:2 * d], mod[:, 2 * d:]
            proj, q_t, v_t = _in_proj(h, scale, shift, cos_t, sin_t, w_in[l].astype(BF16),
                                      attn_width, ATTN_TQ, ATTN_TK)
            y_a = _attention(proj, q_t, v_t,
                             lambda_q1[l].reshape(1, D_HEAD), lambda_k1[l].reshape(1, D_HEAD),
                             lambda_q2[l].reshape(1, D_HEAD), lambda_k2[l].reshape(1, D_HEAD),
                             subln_gain[l].reshape(1, D_VALUE), n_heads, _lambda_init(l),
                             ATTN_TQ, ATTN_TK)
            h = _out_stage(y_a, proj, h, gate, conv_w[l], w_proj_a[l].astype(BF16),
                           w_proj_b[l].astype(BF16), w_out[l].astype(BF16),
                           ln_gain[l].reshape(1, d), ln_bias[l].reshape(1, d), alpha, seg_off)
        outs.append(h)
    return outs[0][None] if batch == 1 else jnp.stack(outs, axis=0)
```
